```python
import math
import jax, jax.numpy as jnp
from jax import lax
import numpy as np


D_MODEL = 1024
BATCH = 4
SEQ = 4096
DEPTH = 1
DEC_BATCH = 128
DEC_SEQ = 1
PAST_LEN = 2048
PAGE_SIZE = 128

N_META = 16
MIX_WIDTH = D_MODEL
H_M = 4
D_V_M = MIX_WIDTH // 2 // H_M
D_QK_M = D_V_M // 2
H_DA = 4
D_V_DA = MIX_WIDTH // 2 // H_DA
D_QK_DA = D_V_DA // 2
CHUNK = 128
Q_BLOCK = 128
N_GROUPS = 4
EXPERTS_PER_GROUP = 8
N_EXPERTS = N_GROUPS * EXPERTS_PER_GROUP
TOP_K_IN_GROUP = 2
D_EXPERT = D_MODEL // 4
EPS = 1e-6
FORGET_BIAS = 3.0
PROJ_SIZES = (H_M * D_QK_M, H_M * D_QK_M, H_M * D_V_M, H_M * D_V_M, H_M, H_M,
              H_DA * 2 * D_QK_DA, H_DA * 2 * D_QK_DA, H_DA * D_V_DA)
PROJ_WIDTH = sum(PROJ_SIZES)

kernel_name = 'hymba_mlstm_diffattn_hmoe_step'


def split_points():
    return tuple(int(o) for o in np.cumsum(PROJ_SIZES)[:-1])


def rms_norm(x, g):
    x32 = x.astype(jnp.float32)
    return x32 * lax.rsqrt(jnp.mean(x32 * x32, axis=-1, keepdims=True) + EPS) * g.astype(jnp.float32)


def split_maps(rows):
    r = rows.reshape(rows.shape[0], rows.shape[1], H_DA, 2, D_QK_DA)
    return r[:, :, :, 0].transpose(0, 2, 1, 3), r[:, :, :, 1].transpose(0, 2, 1, 3)


def mixer_inputs(x, g, w_in, b_gates):
    B, T = x.shape[0], x.shape[1]
    h = rms_norm(x, g).astype(x.dtype)
    p = h @ w_in
    mq, mk, mv, mo, mi, mf, dq, dk, dv = jnp.split(p, split_points(), axis=-1)

    def heads(a, hd):
        return a.reshape(B, T, -1, hd).transpose(0, 2, 1, 3).astype(jnp.float32)

    q = heads(mq, D_QK_M)
    k = heads(mk, D_QK_M) * (D_QK_M ** -0.5)
    v = heads(mv, D_V_M)
    gates = jnp.concatenate([mi, mf], axis=-1).astype(jnp.float32) + b_gates.astype(jnp.float32)
    ig = gates[..., :H_M].transpose(0, 2, 1)
    lf = jax.nn.log_sigmoid(gates[..., H_M:]).transpose(0, 2, 1)
    k_rows = dk.reshape(B, T, H_DA, 2 * D_QK_DA)
    v_rows = dv.reshape(B, T, H_DA, D_V_DA)
    q1, q2 = split_maps(dq.reshape(B, T, H_DA, 2 * D_QK_DA))
    return (q, k, v, ig, lf), mo, q1, q2, k_rows, v_rows


def mlstm_chunk(state, xs):
    C, n, m = state
    q, k, v, ig, lf = xs
    L = q.shape[2]
    b = jnp.cumsum(lf, axis=-1)
    causal = jnp.arange(L)[:, None] >= jnp.arange(L)[None, :]
    log_w = jnp.where(causal, b[..., :, None] - b[..., None, :] + ig[..., None, :], -jnp.inf)
    log_inter = b + m[..., None]
    m_row = jnp.maximum(log_inter, jnp.max(log_w, axis=-1))
    w_intra = jnp.exp(log_w - m_row[..., None])
    w_inter = jnp.exp(log_inter - m_row)
    s = jnp.einsum('bhtd,bhsd->bhts', q, k) * w_intra
    num = jnp.einsum('bhts,bhsv->bhtv', s, v) + w_inter[..., None] * jnp.einsum('bhvd,bhtd->bhtv', C, q)
    den = jnp.sum(s, axis=-1) + w_inter * jnp.einsum('bhd,bhtd->bht', n, q)
    h = num / jnp.maximum(jnp.abs(den), jnp.exp(-m_row))[..., None]
    b_last = b[..., -1]
    log_ws = b_last[..., None] - b + ig
    m_new = jnp.maximum(m + b_last, jnp.max(log_ws, axis=-1))
    ws = jnp.exp(log_ws - m_new[..., None])
    wc = jnp.exp(m + b_last - m_new)
    C_new = wc[..., None, None] * C + jnp.einsum('bhs,bhsv,bhsd->bhvd', ws, v, k)
    n_new = wc[..., None] * n + jnp.einsum('bhs,bhsd->bhd', ws, k)
    return (C_new, n_new, m_new), h


def mlstm_prompt(q, k, v, ig, lf):
    B, H = q.shape[0], q.shape[1]
    init = (jnp.zeros((B, H, D_V_M, D_QK_M), jnp.float32),
            jnp.zeros((B, H, D_QK_M), jnp.float32),
            jnp.zeros((B, H), jnp.float32))
    st, h_meta = mlstm_chunk(init, tuple(a[:, :, :N_META] for a in (q, k, v, ig, lf)))
    n_chunks = SEQ // CHUNK

    def to_chunks(a):
        r = a[:, :, N_META:]
        return jnp.moveaxis(r.reshape(B, H, n_chunks, CHUNK, *r.shape[3:]), 2, 0)

    st, h_real = lax.scan(mlstm_chunk, st, tuple(to_chunks(a) for a in (q, k, v, ig, lf)))
    h_real = jnp.moveaxis(h_real, 0, 2).reshape(B, H, SEQ, D_V_M)
    return jnp.concatenate([h_meta, h_real], axis=2), st


def diff_attn(q1, q2, k1, k2, v, q_pos, k_pos, lam):
    scale = D_QK_DA ** -0.5
    mask = k_pos[None, :] <= q_pos[:, None]
    s1 = jnp.einsum('bhqd,bhkd->bhqk', q1, k1).astype(jnp.float32) * scale
    s2 = jnp.einsum('bhqd,bhkd->bhqk', q2, k2).astype(jnp.float32) * scale
    a1 = jax.nn.softmax(jnp.where(mask, s1, -jnp.inf), axis=-1)
    a2 = jax.nn.softmax(jnp.where(mask, s2, -jnp.inf), axis=-1)
    return jnp.einsum('bhqk,bhkv->bhqv', a1 - lam * a2, v)


def diff_attn_prompt(q1, q2, k1, k2, v, lam):
    B, H = q1.shape[0], q1.shape[1]
    T = N_META + SEQ
    k_pos = jnp.arange(T)
    out_meta = diff_attn(q1[:, :, :N_META], q2[:, :, :N_META], k1[:, :, :N_META], k2[:, :, :N_META],
                         v[:, :, :N_META], jnp.arange(N_META), jnp.arange(N_META), lam)
    n_blocks = SEQ // Q_BLOCK

    def to_blocks(a):
        return jnp.moveaxis(a[:, :, N_META:].reshape(B, H, n_blocks, Q_BLOCK, D_QK_DA), 2, 0)

    starts = N_META + jnp.arange(n_blocks) * Q_BLOCK

    def one_block(args):
        q1b, q2b, start = args
        return diff_attn(q1b, q2b, k1, k2, v, start + jnp.arange(Q_BLOCK), k_pos, lam)

    out_real = lax.map(one_block, (to_blocks(q1), to_blocks(q2), starts))
    out_real = jnp.moveaxis(out_real, 0, 2).reshape(B, H, SEQ, D_V_DA)
    return jnp.concatenate([out_meta, out_real], axis=2)


def mixer_output(h_m, mo, a, g_m, g_da, lam_init, w_out, dtype):
    B, T = mo.shape[0], mo.shape[1]
    hm = rms_norm(h_m, g_m).transpose(0, 2, 1, 3).reshape(B, T, H_M * D_V_M) * jax.nn.sigmoid(mo.astype(jnp.float32))
    ha = (rms_norm(a, g_da) * (1.0 - lam_init)).transpose(0, 2, 1, 3).reshape(B, T, H_DA * D_V_DA)
    return jnp.concatenate([hm, ha], axis=-1).astype(dtype) @ w_out


def moe_ffn(h, w_group, b_group, w_router, b_router, w_gate, w_up, w_down):
    N = h.shape[0]
    g_probs = jax.nn.softmax((h @ w_group).astype(jnp.float32) + b_group.astype(jnp.float32), axis=-1)
    g_idx = jnp.argmax(g_probs, axis=-1)
    g_p = jnp.max(g_probs, axis=-1, keepdims=True)
    e_logits = ((h @ w_router).astype(jnp.float32) + b_router.astype(jnp.float32)).reshape(N, N_GROUPS, EXPERTS_PER_GROUP)
    e_in = jnp.einsum('ng,nge->ne', jax.nn.one_hot(g_idx, N_GROUPS, dtype=jnp.float32), e_logits)
    top_v, top_i = lax.top_k(e_in, TOP_K_IN_GROUP)
    top_p = jax.nn.softmax(top_v, axis=-1) * g_p
    expert_id = g_idx[:, None] * EXPERTS_PER_GROUP + top_i
    gates = jnp.sum(jax.nn.one_hot(expert_id, N_EXPERTS, dtype=jnp.float32) * top_p[..., None], axis=1)
    y = jnp.zeros((N, h.shape[1]), jnp.float32)
    for e in range(N_EXPERTS):
        act = jax.nn.silu(h @ w_gate[e]) * (h @ w_up[e])
        y = y + gates[:, e:e + 1] * (act @ w_down[e]).astype(jnp.float32)
    return y.astype(h.dtype)


def setup_inputs(seed: int = 0) -> dict:
    key = jax.random.key(seed)
    ks = jax.random.split(key, 32)
    n_pages = PAST_LEN // PAGE_SIZE
    n_pool = (DEC_BATCH * n_pages * 5) // 4
    nrm = jax.random.normal
    f32 = jnp.float32
    page_table = jax.random.permutation(ks[7], n_pool)[:DEC_BATCH * n_pages].reshape(DEC_BATCH, n_pages).astype(jnp.int32)
    b_gates = jnp.concatenate([0.1 * nrm(ks[11], (DEPTH, H_M), f32),
                               FORGET_BIAS + 0.5 * nrm(ks[12], (DEPTH, H_M), f32)], axis=-1)
    return {
        'x_prompt': nrm(ks[0], (BATCH, SEQ, D_MODEL), f32),
        'x_sample': nrm(ks[1], (DEC_BATCH, DEC_SEQ, D_MODEL), f32),
        'cache_k': nrm(ks[2], (DEPTH, n_pool, PAGE_SIZE, H_DA, 2 * D_QK_DA), f32),
        'cache_v': nrm(ks[3], (DEPTH, n_pool, PAGE_SIZE, H_DA, D_V_DA), f32),
        'state_C': nrm(ks[4], (DEPTH, DEC_BATCH, H_M, D_V_M, D_QK_M), f32),
        'state_n': nrm(ks[5], (DEPTH, DEC_BATCH, H_M, D_QK_M), f32),
        'state_m': nrm(ks[6], (DEPTH, DEC_BATCH, H_M), f32),
        'page_table': page_table,
        'meta_tokens': nrm(ks[8], (N_META, D_MODEL), f32),
        'norm_mix': 1.0 + 0.02 * nrm(ks[9], (DEPTH, D_MODEL), f32),
        'w_in': nrm(ks[10], (DEPTH, D_MODEL, PROJ_WIDTH), f32) * D_MODEL ** -0.5,
        'b_gates': b_gates,
        'head_norm_m': 1.0 + 0.02 * nrm(ks[13], (DEPTH, D_V_M), f32),
        'lambda_q1': 0.1 * nrm(ks[14], (DEPTH, D_QK_DA), f32),
        'lambda_k1': 0.1 * nrm(ks[15], (DEPTH, D_QK_DA), f32),
        'lambda_q2': 0.1 * nrm(ks[16], (DEPTH, D_QK_DA), f32),
        'lambda_k2': 0.1 * nrm(ks[17], (DEPTH, D_QK_DA), f32),
        'head_norm_da': 1.0 + 0.02 * nrm(ks[18], (DEPTH, D_V_DA), f32),
        'w_out': nrm(ks[19], (DEPTH, MIX_WIDTH, D_MODEL), f32) * MIX_WIDTH ** -0.5,
        'norm_ffn': 1.0 + 0.02 * nrm(ks[20], (DEPTH, D_MODEL), f32),
        'w_group': nrm(ks[21], (DEPTH, D_MODEL, N_GROUPS), f32) * D_MODEL ** -0.5,
        'b_group': 0.01 * nrm(ks[22], (DEPTH, N_GROUPS), f32),
        'w_router': nrm(ks[23], (DEPTH, D_MODEL, N_EXPERTS), f32) * D_MODEL ** -0.5,
        'b_router': 0.01 * nrm(ks[24], (DEPTH, N_EXPERTS), f32),
        'w_gate_e': nrm(ks[25], (DEPTH, N_EXPERTS, D_MODEL, D_EXPERT), f32) * D_MODEL ** -0.5,
        'w_up_e': nrm(ks[26], (DEPTH, N_EXPERTS, D_MODEL, D_EXPERT), f32) * D_MODEL ** -0.5,
        'w_down_e': nrm(ks[27], (DEPTH, N_EXPERTS, D_EXPERT, D_MODEL), f32) * D_EXPERT ** -0.5,
        'norm_final': 1.0 + 0.02 * nrm(ks[28], (D_MODEL,), f32),
    }


def reference(x_prompt, x_sample, cache_k, cache_v, state_C, state_n, state_m, page_table,
              meta_tokens, norm_mix, w_in, b_gates, head_norm_m, lambda_q1, lambda_k1, lambda_q2,
              lambda_k2, head_norm_da, w_out, norm_ffn, w_group, b_group, w_router, b_router,
              w_gate_e, w_up_e, w_down_e, norm_final):
    T = N_META + SEQ
    xp = jnp.concatenate([jnp.broadcast_to(meta_tokens.astype(x_prompt.dtype)[None], (BATCH, N_META, D_MODEL)), x_prompt], axis=1)
    xs = x_sample
    kp_l, vp_l, Cp_l, np_l, mp_l = [], [], [], [], []
    ks_l, vs_l, Cs_l, ns_l, ms_l = [], [], [], [], []
    for l in range(DEPTH):
        lam_init = 0.8 - 0.6 * math.exp(-0.3 * l)
        lam = (jnp.exp(jnp.sum(lambda_q1[l].astype(jnp.float32) * lambda_k1[l].astype(jnp.float32)))
               - jnp.exp(jnp.sum(lambda_q2[l].astype(jnp.float32) * lambda_k2[l].astype(jnp.float32))) + lam_init)

        m_in, mo_p, q1p, q2p, kr_p, vr_p = mixer_inputs(xp, norm_mix[l], w_in[l], b_gates[l])
        hm_p, (C_p, n_p, m_p) = mlstm_prompt(*m_in)
        k1p, k2p = split_maps(kr_p)
        a_p = diff_attn_prompt(q1p, q2p, k1p, k2p, vr_p.transpose(0, 2, 1, 3).astype(jnp.float32), lam)
        xp = xp + mixer_output(hm_p, mo_p, a_p, head_norm_m[l], head_norm_da[l], lam_init, w_out[l], xp.dtype)

        m_in_s, mo_s, q1s, q2s, kr_s, vr_s = mixer_inputs(xs, norm_mix[l], w_in[l], b_gates[l])
        st0 = (state_C[l].astype(jnp.float32), state_n[l].astype(jnp.float32), state_m[l].astype(jnp.float32))
        (C_s, n_s, m_s), hm_s = mlstm_chunk(st0, m_in_s)
        k_past = cache_k[l][page_table].reshape(DEC_BATCH, PAST_LEN, H_DA, 2 * D_QK_DA)
        v_past = cache_v[l][page_table].reshape(DEC_BATCH, PAST_LEN, H_DA, D_V_DA)
        k_all = jnp.concatenate([k_past, kr_s.astype(k_past.dtype)], axis=1)
        v_all = jnp.concatenate([v_past, vr_s.astype(v_past.dtype)], axis=1)
        k1s, k2s = split_maps(k_all)
        a_s = diff_attn(q1s, q2s, k1s, k2s, v_all.transpose(0, 2, 1, 3).astype(jnp.float32),
                        PAST_LEN + jnp.arange(DEC_SEQ), jnp.arange(PAST_LEN + DEC_SEQ), lam)
        xs = xs + mixer_output(hm_s, mo_s, a_s, head_norm_m[l], head_norm_da[l], lam_init, w_out[l], xs.dtype)

        hp = rms_norm(xp, norm_ffn[l]).astype(xp.dtype).reshape(BATCH * T, D_MODEL)
        hs = rms_norm(xs, norm_ffn[l]).astype(xs.dtype).reshape(DEC_BATCH * DEC_SEQ, D_MODEL)
        y = moe_ffn(jnp.concatenate([hp, hs], axis=0), w_group[l], b_group[l], w_router[l], b_router[l],
                    w_gate_e[l], w_up_e[l], w_down_e[l])
        xp = xp + y[:BATCH * T].reshape(BATCH, T, D_MODEL)
        xs = xs + y[BATCH * T:].reshape(DEC_BATCH, DEC_SEQ, D_MODEL)

        kp_l.append(kr_p.astype(cache_k.dtype)); vp_l.append(vr_p.astype(cache_v.dtype))
        Cp_l.append(C_p.astype(state_C.dtype)); np_l.append(n_p.astype(state_n.dtype)); mp_l.append(m_p.astype(state_m.dtype))
        ks_l.append(kr_s.astype(cache_k.dtype)); vs_l.append(vr_s.astype(cache_v.dtype))
        Cs_l.append(C_s.astype(state_C.dtype)); ns_l.append(n_s.astype(state_n.dtype)); ms_l.append(m_s.astype(state_m.dtype))

    y_prompt = rms_norm(xp, norm_final)[:, N_META:].astype(x_prompt.dtype)
    y_sample = rms_norm(xs, norm_final).astype(x_sample.dtype)
    return (y_prompt, y_sample,
            jnp.stack(kp_l), jnp.stack(vp_l), jnp.stack(Cp_l), jnp.stack(np_l), jnp.stack(mp_l),
            jnp.stack(ks_l), jnp.stack(vs_l), jnp.stack(Cs_l), jnp.stack(ns_l), jnp.stack(ms_l))
```

```python
import functools
import math

import jax
import jax.numpy as jnp
from jax import lax
from jax.experimental import pallas as pl
from jax.experimental.pallas import tpu as pltpu

F32 = jnp.float32
BF16 = jnp.bfloat16

EPS = 1e-6
N_META = 16
H_M = 4
D_QK_M = 64
D_V_M = 128
H_DA = 4
D_QK_DA = 64
D_V_DA = 128
CHUNK = 128
N_GROUPS = 4
EXPERTS_PER_GROUP = 8
N_EXPERTS = N_GROUPS * EXPERTS_PER_GROUP
LAM_INIT = 0.8 - 0.6 * math.exp(-0.3 * 0)
PAD_LOG_GATE = -1e30
ROUTER_LANES = 128

VMEM_LIMIT = 56 * 1024 * 1024


def _cparams(n_axes, vmem=None):
    return pltpu.CompilerParams(dimension_semantics=("arbitrary",) * n_axes,
                                vmem_limit_bytes=vmem if vmem else VMEM_LIMIT)


def _dot(a, b):
    return jnp.dot(a, b, preferred_element_type=F32)


def _dot_nt(a, b):
    return lax.dot_general(a, b, (((1,), (1,)), ((), ())), preferred_element_type=F32)


def _rms(x, g):
    return x * lax.rsqrt(jnp.mean(x * x, axis=-1, keepdims=True) + EPS) * g


def _log_sigmoid(x):
    return jnp.minimum(x, 0.0) - jnp.log1p(jnp.exp(-jnp.abs(x)))


def _sigmoid(x):
    return 1.0 / (1.0 + jnp.exp(-x))


_P_MQ, _P_MV, _P_MO, _P_DQ, _P_DK, _P_DV = 0, 256, 768, 1280, 1792, 2304
_P_WIDTH = 2816


def _inproj_prompt_kernel(x_ref, g_ref, w_ref, wkT_ref, wgT_ref, bg_ref,
                          mq_ref, mv_ref, mo_ref, dq_ref, dk_ref, dv_ref, dkb_ref, dvb_ref, kT_ref, gl_ref):
    h = _rms(x_ref[...], g_ref[...]).astype(BF16)
    mq_ref[...] = _dot(h, w_ref[:, _P_MQ:_P_MQ + 256]).astype(BF16)
    mv_ref[...] = _dot(h, w_ref[:, _P_MV:_P_MV + 512]).astype(BF16)
    mo_ref[...] = _dot(h, w_ref[:, _P_MO:_P_MO + 512])
    dq_ref[...] = (_dot(h, w_ref[:, _P_DQ:_P_DQ + 512]) * (D_QK_DA ** -0.5)).astype(BF16)
    dk = _dot(h, w_ref[:, _P_DK:_P_DK + 512])
    dk_ref[...] = dk
    dkb_ref[...] = dk.astype(BF16)
    dv = _dot(h, w_ref[:, _P_DV:_P_DV + 512])
    dv_ref[...] = dv
    dvb_ref[...] = dv.astype(BF16)
    kT_ref[...] = (_dot_nt(wkT_ref[...], h) * (D_QK_M ** -0.5)).astype(BF16)
    gt = _dot_nt(wgT_ref[...], h)[0:8, :] + bg_ref[...]
    row = lax.broadcasted_iota(jnp.int32, gt.shape, 0)
    gl_ref[...] = jnp.where(row < H_M, gt, _log_sigmoid(gt))


def _inproj_prompt(x, g, w, wkT, wgT, bg, n_batch, t_len, tm):
    rows, d = x.shape
    tiles_per_b = t_len // tm
    row_spec = lambda c: pl.BlockSpec((tm, c), lambda i: (i, 0))
    full = lambda a: pl.BlockSpec(a.shape, lambda i: (0,) * a.ndim)
    t_spec = lambda c: pl.BlockSpec((None, c, tm), lambda i: (i // tiles_per_b, 0, i % tiles_per_b))
    out_shape = (
        jax.ShapeDtypeStruct((rows, 256), BF16),
        jax.ShapeDtypeStruct((rows, 512), BF16),
        jax.ShapeDtypeStruct((rows, 512), F32),
        jax.ShapeDtypeStruct((rows, 512), BF16),
        jax.ShapeDtypeStruct((rows, 512), F32),
        jax.ShapeDtypeStruct((rows, 512), F32),
        jax.ShapeDtypeStruct((rows, 512), BF16),
        jax.ShapeDtypeStruct((rows, 512), BF16),
        jax.ShapeDtypeStruct((n_batch, 256, t_len), BF16),
        jax.ShapeDtypeStruct((n_batch, 8, t_len), F32),
    )
    out_specs = (row_spec(256), row_spec(512), row_spec(512), row_spec(512), row_spec(512), row_spec(512),
                 row_spec(512), row_spec(512), t_spec(256), t_spec(8))
    return pl.pallas_call(
        _inproj_prompt_kernel,
        grid=(rows // tm,),
        in_specs=[row_spec(d), full(g), full(w), full(wkT), full(wgT), full(bg)],
        out_specs=out_specs,
        out_shape=out_shape,
        compiler_params=_cparams(1),
        name="inproj_prompt",
    )(x, g, w, wkT, wgT, bg)


_S_MQ, _S_MK, _S_MV, _S_MO, _S_DQ, _S_DK, _S_DV, _S_G = 0, 256, 512, 1024, 1536, 2048, 2560, 3072
_S_WIDTH = 3200


def _inproj_sample_kernel(x_ref, g_ref, w_ref, bg_ref,
                          mq_ref, mk_ref, mv_ref, mo_ref, dq_ref, dk_ref, dv_ref, gl_ref):
    h = _rms(x_ref[...], g_ref[...]).astype(BF16)
    mq_ref[...] = _dot(h, w_ref[:, _S_MQ:_S_MQ + 256])
    mk_ref[...] = _dot(h, w_ref[:, _S_MK:_S_MK + 256]) * (D_QK_M ** -0.5)
    mv_ref[...] = _dot(h, w_ref[:, _S_MV:_S_MV + 512])
    mo_ref[...] = _dot(h, w_ref[:, _S_MO:_S_MO + 512])
    dq_ref[...] = _dot(h, w_ref[:, _S_DQ:_S_DQ + 512]) * (D_QK_DA ** -0.5)
    dk_ref[...] = _dot(h, w_ref[:, _S_DK:_S_DK + 512])
    dv_ref[...] = _dot(h, w_ref[:, _S_DV:_S_DV + 512])
    gt = _dot(h, w_ref[:, _S_G:_S_G + 128]) + bg_ref[...]
    lane = lax.broadcasted_iota(jnp.int32, gt.shape, 1)
    gl_ref[...] = jnp.where(lane < H_M, gt, _log_sigmoid(gt))


def _inproj_sample(x, g, w, bg):
    rows, d = x.shape
    full = lambda a: pl.BlockSpec(a.shape, lambda i: (0,) * a.ndim)
    widths = (256, 256, 512, 512, 512, 512, 512, 128)
    return pl.pallas_call(
        _inproj_sample_kernel,
        grid=(1,),
        in_specs=[full(x), full(g), full(w), full(bg)],
        out_specs=tuple(pl.BlockSpec((rows, c), lambda i: (0, 0)) for c in widths),
        out_shape=tuple(jax.ShapeDtypeStruct((rows, c), F32) for c in widths),
        compiler_params=_cparams(1),
        name="inproj_sample",
    )(x, g, w, bg)


def _scan_lanes(x, op, identity):
    lane = lax.broadcasted_iota(jnp.int32, x.shape, 1)
    k = 1
    while k < x.shape[1]:
        shifted = pltpu.roll(x, k, 1)
        x = op(x, jnp.where(lane >= k, shifted, identity))
        k *= 2
    return x


def _mlstm_prompt_kernel(mq_ref, kT_ref, mv_ref, mo_ref, gl_ref, gm_ref, c0_ref, m0_ref,
                         hm_ref, c_ref, m_ref, *, n_batch):
    step = pl.program_id(0)

    @pl.when(step == 0)
    def _():
        c_ref[...] = c0_ref[...]
        m_ref[...] = m0_ref[...]

    L = CHUNK
    t_idx = lax.broadcasted_iota(jnp.int32, (L, L), 0)
    s_idx = lax.broadcasted_iota(jnp.int32, (L, L), 1)
    causal = s_idx <= t_idx
    lane_l = lax.broadcasted_iota(jnp.int32, (L, L), 1)
    ones_blk = jnp.where(lane_l == 0, 1.0, 0.0).astype(BF16)
    gm = gm_ref[...]

    for bb in range(n_batch):
        gl = gl_ref[bb]
        ig = gl[0:H_M]
        lf = gl[H_M:2 * H_M]
        b = _scan_lanes(lf, jnp.add, 0.0)
        r = ig - b
        cm = _scan_lanes(r, jnp.maximum, -jnp.inf)
        a = b + cm
        m = m_ref[bb]
        m_row = jnp.maximum(b + m, a)
        u = b - m_row
        beta = jnp.exp(b + m - m_row)
        einv = jnp.exp(-m_row)
        b_last = jnp.broadcast_to(b[:, L - 1:L], b.shape)
        g = b_last + jnp.broadcast_to(cm[:, L - 1:L], b.shape)
        m_new = jnp.maximum(m + b_last, g)
        wc = jnp.exp(m + b_last - m_new)
        c2 = jnp.exp(g - m_new)
        ws = jnp.exp(b_last + r - g)
        m_ref[bb] = m_new
        pack = jnp.concatenate([u, beta, einv, ws, jnp.zeros((L - 4 * H_M, L), F32)], axis=0)
        packT = pack.T
        for h in range(H_M):
            q = mq_ref[bb, :, h * D_QK_M:(h + 1) * D_QK_M]
            kT = kT_ref[bb, h * D_QK_M:(h + 1) * D_QK_M, :]
            v = mv_ref[bb, :, h * D_V_M:(h + 1) * D_V_M]
            u_col = packT[:, h:h + 1]
            beta_col = packT[:, H_M + h:H_M + h + 1]
            einv_col = packT[:, 2 * H_M + h:2 * H_M + h + 1]
            ws_col = packT[:, 3 * H_M + h:3 * H_M + h + 1]
            w_intra = jnp.where(causal, jnp.exp(r[h:h + 1, :] + u_col), 0.0)
            s = (_dot(q, kT) * w_intra).astype(BF16)
            v_ext = jnp.concatenate([v, ones_blk], axis=1)
            cst = c_ref[bb, h]
            numden = _dot(s, v_ext) + beta_col * _dot(q, cst.astype(BF16))
            num = numden[:, 0:D_V_M]
            den = numden[:, D_V_M:D_V_M + 1]
            hh = num / jnp.maximum(jnp.abs(den), einv_col)
            gate = _sigmoid(mo_ref[bb, :, h * D_V_M:(h + 1) * D_V_M])
            hm_ref[bb, :, h * D_V_M:(h + 1) * D_V_M] = (_rms(hh, gm) * gate).astype(BF16)
            vs = (v.astype(F32) * ws_col).astype(BF16)
            ws_blk = jnp.where(lane_l == 0, ws_col, 0.0).astype(BF16)
            ut = _dot(kT, jnp.concatenate([vs, ws_blk], axis=1))
            c_ref[bb, h] = wc[h:h + 1, 0:1] * cst + c2[h:h + 1, 0:1] * ut


def _mlstm_prompt(mq, kT, mv, mo, gl, gm, c0, m0):
    n_batch, t_len, _ = mq.shape
    n_chunks = t_len // CHUNK
    full = lambda a: pl.BlockSpec(a.shape, lambda c: (0,) * a.ndim)
    return pl.pallas_call(
        functools.partial(_mlstm_prompt_kernel, n_batch=n_batch),
        grid=(n_chunks,),
        in_specs=[
            pl.BlockSpec((n_batch, CHUNK, 256), lambda c: (0, c, 0)),
            pl.BlockSpec((n_batch, 256, CHUNK), lambda c: (0, 0, c)),
            pl.BlockSpec((n_batch, CHUNK, 512), lambda c: (0, c, 0)),
            pl.BlockSpec((n_batch, CHUNK, 512), lambda c: (0, c, 0)),
            pl.BlockSpec((n_batch, 8, CHUNK), lambda c: (0, 0, c)),
            full(gm), full(c0), full(m0),
        ],
        out_specs=(
            pl.BlockSpec((n_batch, CHUNK, 512), lambda c: (0, c, 0)),
            full(c0), full(m0),
        ),
        out_shape=(
            jax.ShapeDtypeStruct((n_batch, t_len, 512), BF16),
            jax.ShapeDtypeStruct(c0.shape, F32),
            jax.ShapeDtypeStruct(m0.shape, F32),
        ),
        compiler_params=_cparams(1),
        name="mlstm_prompt",
    )(mq, kT, mv, mo, gl, gm, c0, m0)


def _lambda_value(lam_ref):
    lp = lam_ref[...]
    s1 = jnp.sum(lp[0:1] * lp[1:2], axis=1, keepdims=True)
    s2 = jnp.sum(lp[2:3] * lp[3:4], axis=1, keepdims=True)
    return jnp.exp(s1) - jnp.exp(s2) + LAM_INIT


def _attn_prompt_kernel(q_ref, k_ref, v_ref, km_ref, vm_ref, lam_ref, gda_ref, o_ref, *, tq, tk):
    i = pl.program_id(2)
    q = q_ref[...]
    lane = lax.broadcasted_iota(jnp.int32, q.shape, 1)
    zero = jnp.zeros_like(q)
    qz = jnp.concatenate([jnp.where(lane < D_QK_DA, q, zero), jnp.where(lane >= D_QK_DA, q, zero)], axis=0)

    s = _dot_nt(qz, km_ref[...])
    m = jnp.max(s, axis=1, keepdims=True)
    p = jnp.exp(s - m)
    l = jnp.sum(p, axis=1, keepdims=True)
    acc = _dot(p.astype(BF16), vm_ref[...])

    def block(j, carry, masked):
        m, l, acc = carry
        start = pl.multiple_of(j * tk, tk)
        kb = k_ref[pl.ds(start, tk), :]
        vb = v_ref[pl.ds(start, tk), :]
        s = _dot_nt(qz, kb)
        if masked:
            row = lax.broadcasted_iota(jnp.int32, s.shape, 0)
            row = jnp.where(row >= tq, row - tq, row) + i * tq
            col = lax.broadcasted_iota(jnp.int32, s.shape, 1) + j * tk
            s = jnp.where(col <= row, s, -jnp.inf)
        m_new = jnp.maximum(m, jnp.max(s, axis=1, keepdims=True))
        alpha = jnp.exp(m - m_new)
        p = jnp.exp(s - m_new)
        l = alpha * l + jnp.sum(p, axis=1, keepdims=True)
        acc = alpha * acc + _dot(p.astype(BF16), vb)
        return m_new, l, acc

    n_full = (i * tq) // tk
    carry = lax.fori_loop(0, n_full, lambda j, c: block(j, c, False), (m, l, acc))
    n_diag = -(-tq // tk)
    for d in range(n_diag):
        carry = block(n_full + d, carry, True)
    m, l, acc = carry

    o = acc / l
    a = o[0:tq] - _lambda_value(lam_ref) * o[tq:2 * tq]
    o_ref[...] = (_rms(a, gda_ref[...]) * (1.0 - LAM_INIT)).astype(BF16)


def _attn_prompt(dq, dkb, dvb, km, vm, lam_p, gda, tq, tk):
    n_batch, t_len, _ = dq.shape
    full = lambda a: pl.BlockSpec(a.shape, lambda b, h, i: (0,) * a.ndim)
    return pl.pallas_call(
        functools.partial(_attn_prompt_kernel, tq=tq, tk=tk),
        grid=(n_batch, H_DA, t_len // tq),
        in_specs=[
            pl.BlockSpec((None, tq, 128), lambda b, h, i: (b, i, h)),
            pl.BlockSpec((None, t_len, 128), lambda b, h, i: (b, 0, h)),
            pl.BlockSpec((None, t_len, 128), lambda b, h, i: (b, 0, h)),
            pl.BlockSpec((N_META, 128), lambda b, h, i: (0, h)),
            pl.BlockSpec((N_META, 128), lambda b, h, i: (0, h)),
            full(lam_p), full(gda),
        ],
        out_specs=pl.BlockSpec((None, tq, 128), lambda b, h, i: (b, i, h)),
        out_shape=jax.ShapeDtypeStruct((n_batch, t_len, 512), BF16),
        compiler_params=_cparams(3),
        name="attn_prompt",
    )(dq, dkb, dvb, km, vm, lam_p, gda)


def _mlstm_sample_kernel(mq_ref, mk_ref, mv_ref, mo_ref, gl_ref, gm_ref, c_ref, n_ref, m_ref,
                         hm_ref, co_ref, no_ref, mo_out_ref, *, tb):
    gl = gl_ref[...]
    m_all = m_ref[...]
    gm = gm_ref[...]
    row8 = lax.broadcasted_iota(jnp.int32, (tb, D_V_M), 0)
    for h in range(H_M):
        q = mq_ref[:, h * D_QK_M:(h + 1) * D_QK_M]
        k = mk_ref[:, h * D_QK_M:(h + 1) * D_QK_M]
        v = mv_ref[:, h * D_V_M:(h + 1) * D_V_M]
        ig = gl[:, h:h + 1]
        lf = gl[:, H_M + h:H_M + h + 1]
        m = m_all[:, h:h + 1]
        n = n_ref[:, h, :]
        m_row = jnp.maximum(lf + m, ig)
        w_intra = jnp.exp(ig - m_row)
        w_inter = jnp.exp(lf + m - m_row)
        s = jnp.sum(q * k, axis=1, keepdims=True) * w_intra
        nq = jnp.sum(n * q, axis=1, keepdims=True)
        qb = q.astype(BF16)
        cq = jnp.zeros((tb, D_V_M), F32)
        for j in range(tb):
            res = _dot_nt(qb, c_ref[j, h].astype(BF16))
            cq = jnp.where(row8 == j, res, cq)
        num = s * v + w_inter * cq
        den = s + w_inter * nq
        hh = num / jnp.maximum(jnp.abs(den), jnp.exp(-m_row))
        gate = _sigmoid(mo_ref[:, h * D_V_M:(h + 1) * D_V_M])
        hm_ref[:, h * D_V_M:(h + 1) * D_V_M] = (_rms(hh, gm) * gate).astype(BF16)
        m_new = jnp.maximum(m + lf, ig)
        ws = jnp.exp(ig - m_new)
        wc = jnp.exp(m + lf - m_new)
        mo_out_ref[:, h:h + 1] = m_new
        no_ref[:, h, :] = wc * n + ws * k
        vT = jnp.concatenate([v * ws, jnp.zeros((D_V_M - tb, D_V_M), F32)], axis=0).T
        for j in range(tb):
            co_ref[j, h] = wc[j:j + 1, :] * c_ref[j, h] + vT[:, j:j + 1] * k[j:j + 1, :]


def _mlstm_sample(mq, mk, mv, mo, gl, gm, c, n, m, tb=8):
    nb = mq.shape[0]
    rows = lambda w: pl.BlockSpec((tb, w), lambda i: (i, 0))
    return pl.pallas_call(
        functools.partial(_mlstm_sample_kernel, tb=tb),
        grid=(nb // tb,),
        in_specs=[rows(256), rows(256), rows(512), rows(512), rows(128),
                  pl.BlockSpec(gm.shape, lambda i: (0, 0)),
                  pl.BlockSpec((tb, H_M, D_V_M, D_QK_M), lambda i: (i, 0, 0, 0)),
                  pl.BlockSpec((tb, H_M, D_QK_M), lambda i: (i, 0, 0)),
                  pl.BlockSpec((tb, H_M), lambda i: (i, 0))],
        out_specs=(rows(512),
                   pl.BlockSpec((tb, H_M, D_V_M, D_QK_M), lambda i: (i, 0, 0, 0)),
                   pl.BlockSpec((tb, H_M, D_QK_M), lambda i: (i, 0, 0)),
                   pl.BlockSpec((tb, H_M), lambda i: (i, 0))),
        out_shape=(jax.ShapeDtypeStruct((nb, 512), BF16),
                   jax.ShapeDtypeStruct(c.shape, F32),
                   jax.ShapeDtypeStruct(n.shape, F32),
                   jax.ShapeDtypeStruct(m.shape, F32)),
        compiler_params=_cparams(1),
        name="mlstm_sample",
    )(mq, mk, mv, mo, gl, gm, c, n, m)


_QROWS = 16


def _attn_decode_kernel(pt_ref, q_ref, ks_ref, vs_ref, lam_ref, gda_ref, *refs, n_pages):
    k_refs = refs[:n_pages]
    v_refs = refs[n_pages:2 * n_pages]
    o_ref = refs[2 * n_pages]
    width = H_DA * 2 * D_QK_DA
    q = q_ref[...]
    rowi = lax.broadcasted_iota(jnp.int32, (_QROWS, width), 0)
    coli = lax.broadcasted_iota(jnp.int32, (_QROWS, width), 1)
    qbd = jnp.where(coli // D_QK_DA == rowi, jnp.broadcast_to(q, (_QROWS, width)), 0.0).astype(BF16)

    scores = [_dot_nt(qbd, k_refs[j][...].astype(BF16)) for j in range(n_pages)]
    ks = ks_ref[...].astype(BF16).astype(F32)
    s_self = jnp.sum(qbd.astype(F32) * ks, axis=1, keepdims=True)
    m = s_self
    for sj in scores:
        m = jnp.maximum(m, jnp.max(sj, axis=1, keepdims=True))
    p_self = jnp.exp(s_self - m)
    l = p_self
    acc = p_self * vs_ref[...]
    for j in range(n_pages):
        p = jnp.exp(scores[j] - m)
        l = l + jnp.sum(p, axis=1, keepdims=True)
        acc = acc + _dot(p.astype(BF16), v_refs[j][...].astype(BF16))
    o = acc / l
    head_of_col = coli // D_V_DA
    o1 = jnp.sum(jnp.where(rowi == 2 * head_of_col, o, 0.0), axis=0, keepdims=True)
    o2 = jnp.sum(jnp.where(rowi == 2 * head_of_col + 1, o, 0.0), axis=0, keepdims=True)
    a = o1 - _lambda_value(lam_ref) * o2
    g = gda_ref[...]
    for h in range(H_DA):
        seg = a[:, h * D_V_DA:(h + 1) * D_V_DA]
        o_ref[:, h * D_V_DA:(h + 1) * D_V_DA] = (_rms(seg, g) * (1.0 - LAM_INIT)).astype(BF16)


def _attn_decode(page_table, q, ks, vs, lam_p, gda, cache_k, cache_v):
    nb = q.shape[0]
    n_pages = page_table.shape[0] // nb
    page = cache_k.shape[1]
    tok = pl.BlockSpec((None, 1, 512), lambda b, pt: (b, 0, 0))
    full = lambda a: pl.BlockSpec(a.shape, lambda b, pt: (0,) * a.ndim)

    def page_spec(j):
        return pl.BlockSpec((None, page, 512), lambda b, pt: (pt[b * n_pages + j], 0, 0))

    grid_spec = pltpu.PrefetchScalarGridSpec(
        num_scalar_prefetch=1,
        grid=(nb,),
        in_specs=[tok, tok, tok, full(lam_p), full(gda)]
                 + [page_spec(j) for j in range(n_pages)] + [page_spec(j) for j in range(n_pages)],
        out_specs=pl.BlockSpec((None, 1, 512), lambda b, pt: (b, 0, 0)),
    )
    return pl.pallas_call(
        functools.partial(_attn_decode_kernel, n_pages=n_pages),
        grid_spec=grid_spec,
        out_shape=jax.ShapeDtypeStruct((nb, 1, 512), BF16),
        compiler_params=_cparams(1),
        name="attn_decode",
    )(page_table, q, ks, vs, lam_p, gda, *([cache_k] * n_pages), *([cache_v] * n_pages))


def _route(logits):
    lane = lax.broadcasted_iota(jnp.int32, logits.shape, 1)
    big = jnp.int32(ROUTER_LANES)
    neg = -jnp.inf
    gl = jnp.where(lane < N_GROUPS, logits, neg)
    gmax = jnp.max(gl, axis=1, keepdims=True)
    g_idx = jnp.min(jnp.where(gl == gmax, lane, big), axis=1, keepdims=True)
    g_p = 1.0 / jnp.sum(jnp.exp(gl - gmax), axis=1, keepdims=True)
    lo = N_GROUPS + EXPERTS_PER_GROUP * g_idx
    in_group = (lane >= lo) & (lane < lo + EXPERTS_PER_GROUP)
    el = jnp.where(in_group, logits, neg)
    v1 = jnp.max(el, axis=1, keepdims=True)
    i1 = jnp.min(jnp.where(el == v1, lane, big), axis=1, keepdims=True)
    el2 = jnp.where(lane == i1, neg, el)
    v2 = jnp.max(el2, axis=1, keepdims=True)
    i2 = jnp.min(jnp.where(el2 == v2, lane, big), axis=1, keepdims=True)
    e2 = jnp.exp(v2 - v1)
    p1 = g_p / (1.0 + e2)
    p2 = g_p * e2 / (1.0 + e2)
    return jnp.where(lane == i1, p1, 0.0) + jnp.where(lane == i2, p2, 0.0)


def _mixout_kernel(hm_ref, ha_ref, x_ref, wo_ref, gf_ref, wr_ref, br_ref, xmid_ref, h2_ref, gates_ref):
    half = hm_ref.shape[1]
    y = _dot(hm_ref[...], wo_ref[0:half, :]) + _dot(ha_ref[...], wo_ref[half:2 * half, :])
    xmid = x_ref[...] + y
    xmid_ref[...] = xmid
    h2 = _rms(xmid, gf_ref[...]).astype(BF16)
    h2_ref[...] = h2
    gates_ref[...] = _route(_dot(h2, wr_ref[...]) + br_ref[...])


def _mixout(hm, ha, x, wo, gf, wr, br, tm):
    rows, d = x.shape
    rs = lambda c: pl.BlockSpec((tm, c), lambda i: (i, 0))
    full = lambda a: pl.BlockSpec(a.shape, lambda i: (0,) * a.ndim)
    return pl.pallas_call(
        _mixout_kernel,
        grid=(rows // tm,),
        in_specs=[rs(512), rs(512), rs(d), full(wo), full(gf), full(wr), full(br)],
        out_specs=(rs(d), rs(d), rs(ROUTER_LANES)),
        out_shape=(jax.ShapeDtypeStruct((rows, d), F32),
                   jax.ShapeDtypeStruct((rows, d), BF16),
                   jax.ShapeDtypeStruct((rows, ROUTER_LANES), F32)),
        compiler_params=_cparams(1),
        name="mixout",
    )(hm, ha, x, wo, gf, wr, br)


def _moe_kernel(h2_ref, gates_ref, xmid_ref, wgu_ref, wd_ref, gfin_ref, y_ref, acc_ref):
    e = pl.program_id(1)

    @pl.when(e == 0)
    def _():
        acc_ref[...] = jnp.zeros_like(acc_ref)

    d_e = wd_ref.shape[0]
    gu = _dot(h2_ref[...], wgu_ref[...])
    gte = gu[:, 0:d_e]
    act = (gte * _sigmoid(gte) * gu[:, d_e:2 * d_e]).astype(BF16)
    o = _dot(act, wd_ref[...])
    gates = gates_ref[...]
    lane = lax.broadcasted_iota(jnp.int32, gates.shape, 1)
    gcol = jnp.sum(jnp.where(lane == e + N_GROUPS, gates, 0.0), axis=1, keepdims=True)
    acc_ref[...] += gcol * o

    @pl.when(e == pl.num_programs(1) - 1)
    def _():
        y_ref[...] = _rms(xmid_ref[...] + acc_ref[...], gfin_ref[...])


def _moe(h2, gates, xmid, wgu, wd, gfin, tm):
    rows, d = xmid.shape
    n_e, _, de2 = wgu.shape
    return pl.pallas_call(
        _moe_kernel,
        grid=(rows // tm, n_e),
        in_specs=[pl.BlockSpec((tm, d), lambda i, e: (i, 0)),
                  pl.BlockSpec((tm, ROUTER_LANES), lambda i, e: (i, 0)),
                  pl.BlockSpec((tm, d), lambda i, e: (i, 0)),
                  pl.BlockSpec((None, d, de2), lambda i, e: (e, 0, 0)),
                  pl.BlockSpec((None, de2 // 2, d), lambda i, e: (e, 0, 0)),
                  pl.BlockSpec(gfin.shape, lambda i, e: (0, 0))],
        out_specs=pl.BlockSpec((tm, d), lambda i, e: (i, 0)),
        out_shape=jax.ShapeDtypeStruct((rows, d), F32),
        scratch_shapes=[pltpu.VMEM((tm, d), F32)],
        compiler_params=_cparams(2),
        name="moe",
    )(h2, gates, xmid, wgu, wd, gfin)


def kernel(x_prompt, x_sample, cache_k, cache_v, state_C, state_n, state_m, page_table, meta_tokens, norm_mix, w_in, b_gates, head_norm_m, lambda_q1, lambda_k1, lambda_q2, lambda_k2, head_norm_da, w_out, norm_ffn, w_group, b_group, w_router, b_router, w_gate_e, w_up_e, w_down_e, norm_final):
    n_batch, seq, d_model = x_prompt.shape
    nb_dec = x_sample.shape[0]
    assert cache_k.shape[0] == 1 and x_sample.shape[1] == 1, "one layer, one new token per sequence"
    n_pages = page_table.shape[1]

    w = w_in[0]
    cuts = [0]
    for wd_ in (256, 256, 512, 512, 4, 4, 512, 512, 512):
        cuts.append(cuts[-1] + wd_)
    w_mq, w_mk, w_mv, w_mo, w_mi, w_mf, w_dq, w_dk, w_dv = (w[:, cuts[i]:cuts[i + 1]] for i in range(9))
    w_gates = jnp.concatenate([w_mi, w_mf], axis=1)
    w_p = jnp.concatenate([w_mq, w_mv, w_mo, w_dq, w_dk, w_dv], axis=1).astype(BF16)
    wkT = w_mk.T.astype(BF16)
    wgT = jnp.pad(w_gates.T, ((0, 8), (0, 0))).astype(BF16)
    w_s = jnp.concatenate([w_mq, w_mk, w_mv, w_mo, w_dq, w_dk, w_dv,
                           jnp.pad(w_gates, ((0, 0), (0, 120)))], axis=1).astype(BF16)
    bg_col = b_gates[0].reshape(8, 1).astype(F32)
    bg_row = jnp.pad(b_gates[0].reshape(1, 8), ((0, 0), (0, 120))).astype(F32)
    g_mix = norm_mix[0].reshape(1, d_model)
    gm = head_norm_m[0].reshape(1, D_V_M)
    gda = head_norm_da[0].reshape(1, D_V_DA)
    lam_p = jnp.stack([lambda_q1[0], lambda_k1[0], lambda_q2[0], lambda_k2[0]]).astype(F32)
    wo = w_out[0].astype(BF16)
    g_ffn = norm_ffn[0].reshape(1, d_model)
    w_r = jnp.pad(jnp.concatenate([w_group[0], w_router[0]], axis=1),
                  ((0, 0), (0, ROUTER_LANES - N_GROUPS - N_EXPERTS))).astype(BF16)
    b_r = jnp.pad(jnp.concatenate([b_group[0], b_router[0]]).reshape(1, -1),
                  ((0, 0), (0, ROUTER_LANES - N_GROUPS - N_EXPERTS))).astype(F32)
    wgu = jnp.concatenate([w_gate_e[0], w_up_e[0]], axis=-1).astype(BF16)
    wdn = w_down_e[0].astype(BF16)
    g_fin = norm_final.reshape(1, d_model)

    xp = x_prompt.reshape(n_batch * seq, d_model)
    (mq, mv, mo, dq, dk, dv, dkb, dvb, kT, gl) = _inproj_prompt(xp, g_mix, w_p, wkT, wgT, bg_col, n_batch, seq, 512)
    x_meta = jnp.pad(meta_tokens.astype(F32), ((0, CHUNK - N_META), (0, 0)))
    (mq_m, mv_m, mo_m, _, dk_m, dv_m, dkb_m, dvb_m, kT_m, gl_m) = _inproj_prompt(
        x_meta, g_mix, w_p, wkT, wgT, bg_col, 1, CHUNK, CHUNK)
    (mq_s, mk_s, mv_s, mo_s, dq_s, dk_s, dv_s, gl_s) = _inproj_sample(
        x_sample.reshape(nb_dec, d_model), g_mix, w_s, bg_row)

    lane = jnp.arange(CHUNK)[None, None, :]
    pad_gate = jnp.where(jnp.arange(8)[None, :, None] < H_M, PAD_LOG_GATE, 0.0)
    gl_m = jnp.where(lane < N_META, gl_m, pad_gate)
    c_zero = jnp.zeros((1, H_M, D_QK_M, 256), F32)
    m_zero = jnp.zeros((1, H_M, CHUNK), F32)
    _, c_meta, m_meta = _mlstm_prompt(mq_m.reshape(1, CHUNK, 256), kT_m, mv_m.reshape(1, CHUNK, 512),
                                      mo_m.reshape(1, CHUNK, 512), gl_m, gm, c_zero, m_zero)
    hm_p, c_fin, m_fin = _mlstm_prompt(
        mq.reshape(n_batch, seq, 256), kT, mv.reshape(n_batch, seq, 512), mo.reshape(n_batch, seq, 512), gl, gm,
        jnp.broadcast_to(c_meta, (n_batch,) + c_meta.shape[1:]),
        jnp.broadcast_to(m_meta, (n_batch,) + m_meta.shape[1:]))
    hm_s, c_s, n_s, m_s = _mlstm_sample(mq_s, mk_s, mv_s, mo_s, gl_s, gm,
                                        state_C[0].astype(F32), state_n[0].astype(F32), state_m[0].astype(F32))

    ha_p = _attn_prompt(dq.reshape(n_batch, seq, 512), dkb.reshape(n_batch, seq, 512),
                        dvb.reshape(n_batch, seq, 512), dkb_m[:N_META], dvb_m[:N_META], lam_p, gda, 256, 512)
    n_pool, page = cache_k.shape[1], cache_k.shape[2]
    ha_s = _attn_decode(page_table.reshape(-1).astype(jnp.int32),
                        dq_s.reshape(nb_dec, 1, 512), dk_s.reshape(nb_dec, 1, 512), dv_s.reshape(nb_dec, 1, 512),
                        lam_p, gda, cache_k[0].reshape(n_pool, page, 512), cache_v[0].reshape(n_pool, page, 512))

    xmid_p, h2_p, gates_p = _mixout(hm_p.reshape(n_batch * seq, 512), ha_p.reshape(n_batch * seq, 512), xp,
                                    wo, g_ffn, w_r, b_r, 512)
    xmid_s, h2_s, gates_s = _mixout(hm_s, ha_s.reshape(nb_dec, 512), x_sample.reshape(nb_dec, d_model),
                                    wo, g_ffn, w_r, b_r, nb_dec)
    y_p = _moe(h2_p, gates_p, xmid_p, wgu, wdn, g_fin, 1024)
    y_s = _moe(h2_s, gates_s, xmid_s, wgu, wdn, g_fin, nb_dec)

    def with_meta(meta_rows, real_rows):
        meta_b = jnp.broadcast_to(meta_rows[None, :N_META], (n_batch, N_META, 512))
        full = jnp.concatenate([meta_b, real_rows.reshape(n_batch, seq, 512)], axis=1)
        return full.reshape(1, n_batch, N_META + seq, H_DA, 128)

    k_prompt = with_meta(dk_m, dk).astype(cache_k.dtype)
    v_prompt = with_meta(dv_m, dv).astype(cache_v.dtype)
    C_prompt = jnp.swapaxes(c_fin[:, :, :, :D_V_M], 2, 3)[None].astype(state_C.dtype)
    n_prompt = c_fin[:, :, :, D_V_M][None].astype(state_n.dtype)
    m_prompt = m_fin[:, :, 0][None].astype(state_m.dtype)
    return (y_p.reshape(n_batch, seq, d_model).astype(x_prompt.dtype),
            y_s.reshape(nb_dec, 1, d_model).astype(x_sample.dtype),
            k_prompt, v_prompt, C_prompt, n_prompt, m_prompt,
            dk_s.reshape(1, nb_dec, 1, H_DA, 128).astype(cache_k.dtype),
            dv_s.reshape(1, nb_dec, 1, H_DA, 128).astype(cache_v.dtype),
            c_s[None].astype(state_C.dtype), n_s[None].astype(state_n.dtype), m_s[None].astype(state_m.dtype))
```

```python
import functools
import math

import jax
import jax.numpy as jnp
from jax import lax
from jax.experimental import pallas as pl
from jax.experimental.pallas import tpu as pltpu

F32 = jnp.float32
BF16 = jnp.bfloat16

EPS = 1e-6
N_META = 16
H_M = 4
D_QK_M = 64
D_V_M = 128
H_DA = 4
D_QK_DA = 64
D_V_DA = 128
CHUNK = 128
N_GROUPS = 4
EXPERTS_PER_GROUP = 8
N_EXPERTS = N_GROUPS * EXPERTS_PER_GROUP
LAM_INIT = 0.8 - 0.6 * math.exp(-0.3 * 0)
PAD_LOG_GATE = -1e30
ROUTER_LANES = 128

VMEM_LIMIT = 56 * 1024 * 1024


def _cparams(n_axes, vmem=None):
    return pltpu.CompilerParams(dimension_semantics=("arbitrary",) * n_axes,
                                vmem_limit_bytes=vmem if vmem else VMEM_LIMIT)


def _dot(a, b):
    return jnp.dot(a, b, preferred_element_type=F32)


def _dot_nt(a, b):
    return lax.dot_general(a, b, (((1,), (1,)), ((), ())), preferred_element_type=F32)


def _rms(x, g):
    return x * lax.rsqrt(jnp.mean(x * x, axis=-1, keepdims=True) + EPS) * g


def _log_sigmoid(x):
    return jnp.minimum(x, 0.0) - jnp.log1p(jnp.exp(-jnp.abs(x)))


def _sigmoid(x):
    return 1.0 / (1.0 + jnp.exp(-x))


_P_MQ, _P_MV, _P_MO, _P_DQ, _P_DK, _P_DV = 0, 256, 768, 1280, 1792, 2304
_P_WIDTH = 2816


def _inproj_prompt_kernel(x_ref, g_ref, w_ref, wkT_ref, wgT_ref, bg_ref,
                          mq_ref, mv_ref, mo_ref, dq_ref, dk_ref, dv_ref, dkb_ref, dvb_ref, kT_ref, gl_ref):
    h = _rms(x_ref[...], g_ref[...]).astype(BF16)
    mq_ref[...] = _dot(h, w_ref[:, _P_MQ:_P_MQ + 256]).astype(BF16)
    mv_ref[...] = _dot(h, w_ref[:, _P_MV:_P_MV + 512]).astype(BF16)
    mo_ref[...] = _dot(h, w_ref[:, _P_MO:_P_MO + 512])
    dq_ref[...] = (_dot(h, w_ref[:, _P_DQ:_P_DQ + 512]) * (D_QK_DA ** -0.5)).astype(BF16)
    tm = x_ref.shape[0]
    dk = _dot(h, w_ref[:, _P_DK:_P_DK + 512])
    dkb_ref[...] = dk.astype(BF16)
    dv = _dot(h, w_ref[:, _P_DV:_P_DV + 512])
    dvb_ref[...] = dv.astype(BF16)
    for hd in range(H_DA):
        dk_ref[pl.ds(hd, tm, stride=H_DA), :] = dk[:, hd * 128:(hd + 1) * 128]
        dv_ref[pl.ds(hd, tm, stride=H_DA), :] = dv[:, hd * 128:(hd + 1) * 128]
    kT_ref[...] = (_dot_nt(wkT_ref[...], h) * (D_QK_M ** -0.5)).astype(BF16)
    gt = _dot_nt(wgT_ref[...], h)[0:8, :] + bg_ref[...]
    row = lax.broadcasted_iota(jnp.int32, gt.shape, 0)
    gl_ref[...] = jnp.where(row < H_M, gt, _log_sigmoid(gt))


def _inproj_prompt(x, g, w, wkT, wgT, bg, n_batch, t_len, tm):
    rows, d = x.shape
    tiles_per_b = t_len // tm
    row_spec = lambda c: pl.BlockSpec((tm, c), lambda i: (i, 0))
    full = lambda a: pl.BlockSpec(a.shape, lambda i: (0,) * a.ndim)
    t_spec = lambda c: pl.BlockSpec((None, c, tm), lambda i: (i // tiles_per_b, 0, i % tiles_per_b))
    out_shape = (
        jax.ShapeDtypeStruct((rows, 256), BF16),
        jax.ShapeDtypeStruct((rows, 512), BF16),
        jax.ShapeDtypeStruct((rows, 512), F32),
        jax.ShapeDtypeStruct((rows, 512), BF16),
        jax.ShapeDtypeStruct((rows * H_DA, 128), F32),
        jax.ShapeDtypeStruct((rows * H_DA, 128), F32),
        jax.ShapeDtypeStruct((rows, 512), BF16),
        jax.ShapeDtypeStruct((rows, 512), BF16),
        jax.ShapeDtypeStruct((n_batch, 256, t_len), BF16),
        jax.ShapeDtypeStruct((n_batch, 8, t_len), F32),
    )
    head_rows = pl.BlockSpec((tm * H_DA, 128), lambda i: (i, 0))
    out_specs = (row_spec(256), row_spec(512), row_spec(512), row_spec(512), head_rows, head_rows,
                 row_spec(512), row_spec(512), t_spec(256), t_spec(8))
    return pl.pallas_call(
        _inproj_prompt_kernel,
        grid=(rows // tm,),
        in_specs=[row_spec(d), full(g), full(w), full(wkT), full(wgT), full(bg)],
        out_specs=out_specs,
        out_shape=out_shape,
        compiler_params=_cparams(1),
        name="inproj_prompt",
    )(x, g, w, wkT, wgT, bg)


_S_MQ, _S_MK, _S_MV, _S_MO, _S_DQ, _S_DK, _S_DV, _S_G = 0, 256, 512, 1024, 1536, 2048, 2560, 3072
_S_WIDTH = 3200


def _inproj_sample_kernel(x_ref, g_ref, w_ref, bg_ref,
                          mq_ref, mk_ref, mv_ref, mo_ref, dq_ref, dk_ref, dv_ref, gl_ref):
    h = _rms(x_ref[...], g_ref[...]).astype(BF16)
    mq_ref[...] = _dot(h, w_ref[:, _S_MQ:_S_MQ + 256])
    mk_ref[...] = _dot(h, w_ref[:, _S_MK:_S_MK + 256]) * (D_QK_M ** -0.5)
    mv_ref[...] = _dot(h, w_ref[:, _S_MV:_S_MV + 512])
    mo_ref[...] = _dot(h, w_ref[:, _S_MO:_S_MO + 512])
    dq_ref[...] = _dot(h, w_ref[:, _S_DQ:_S_DQ + 512]) * (D_QK_DA ** -0.5)
    dk_ref[...] = _dot(h, w_ref[:, _S_DK:_S_DK + 512])
    dv_ref[...] = _dot(h, w_ref[:, _S_DV:_S_DV + 512])
    gt = _dot(h, w_ref[:, _S_G:_S_G + 128]) + bg_ref[...]
    lane = lax.broadcasted_iota(jnp.int32, gt.shape, 1)
    gl_ref[...] = jnp.where(lane < H_M, gt, _log_sigmoid(gt))


def _inproj_sample(x, g, w, bg):
    rows, d = x.shape
    full = lambda a: pl.BlockSpec(a.shape, lambda i: (0,) * a.ndim)
    widths = (256, 256, 512, 512, 512, 512, 512, 128)
    return pl.pallas_call(
        _inproj_sample_kernel,
        grid=(1,),
        in_specs=[full(x), full(g), full(w), full(bg)],
        out_specs=tuple(pl.BlockSpec((rows, c), lambda i: (0, 0)) for c in widths),
        out_shape=tuple(jax.ShapeDtypeStruct((rows, c), F32) for c in widths),
        compiler_params=_cparams(1),
        name="inproj_sample",
    )(x, g, w, bg)


def _scan_lanes(x, op, identity):
    lane = lax.broadcasted_iota(jnp.int32, x.shape, 1)
    k = 1
    while k < x.shape[1]:
        shifted = pltpu.roll(x, k, 1)
        x = op(x, jnp.where(lane >= k, shifted, identity))
        k *= 2
    return x


def _mlstm_prompt_kernel(mq_ref, kT_ref, mv_ref, mo_ref, gl_ref, gm_ref, c0_ref, m0_ref,
                         hm_ref, c_ref, m_ref, *, n_batch):
    step = pl.program_id(0)

    @pl.when(step == 0)
    def _():
        c_ref[...] = c0_ref[...]
        m_ref[...] = m0_ref[...]

    L = CHUNK
    t_idx = lax.broadcasted_iota(jnp.int32, (L, L), 0)
    s_idx = lax.broadcasted_iota(jnp.int32, (L, L), 1)
    causal = s_idx <= t_idx
    lane_l = lax.broadcasted_iota(jnp.int32, (L, L), 1)
    ones_blk = jnp.where(lane_l == 0, 1.0, 0.0).astype(BF16)
    gm = gm_ref[...]

    for bb in range(n_batch):
        gl = gl_ref[bb]
        ig = gl[0:H_M]
        lf = gl[H_M:2 * H_M]
        b = _scan_lanes(lf, jnp.add, 0.0)
        r = ig - b
        cm = _scan_lanes(r, jnp.maximum, -jnp.inf)
        a = b + cm
        m = m_ref[bb]
        m_row = jnp.maximum(b + m, a)
        u = b - m_row
        beta = jnp.exp(b + m - m_row)
        einv = jnp.exp(-m_row)
        b_last = jnp.broadcast_to(b[:, L - 1:L], b.shape)
        g = b_last + jnp.broadcast_to(cm[:, L - 1:L], b.shape)
        m_new = jnp.maximum(m + b_last, g)
        wc = jnp.exp(m + b_last - m_new)
        c2 = jnp.exp(g - m_new)
        ws = jnp.exp(b_last + r - g)
        m_ref[bb] = m_new
        pack = jnp.concatenate([u, beta, einv, ws, jnp.zeros((L - 4 * H_M, L), F32)], axis=0)
        packT = pack.T
        for h in range(H_M):
            q = mq_ref[bb, :, h * D_QK_M:(h + 1) * D_QK_M]
            kT = kT_ref[bb, h * D_QK_M:(h + 1) * D_QK_M, :]
            v = mv_ref[bb, :, h * D_V_M:(h + 1) * D_V_M]
            u_col = packT[:, h:h + 1]
            beta_col = packT[:, H_M + h:H_M + h + 1]
            einv_col = packT[:, 2 * H_M + h:2 * H_M + h + 1]
            ws_col = packT[:, 3 * H_M + h:3 * H_M + h + 1]
            w_intra = jnp.where(causal, jnp.exp(r[h:h + 1, :] + u_col), 0.0)
            s = (_dot(q, kT) * w_intra).astype(BF16)
            v_ext = jnp.concatenate([v, ones_blk], axis=1)
            cst = c_ref[bb, h]
            numden = _dot(s, v_ext) + beta_col * _dot(q, cst.astype(BF16))
            num = numden[:, 0:D_V_M]
            den = numden[:, D_V_M:D_V_M + 1]
            hh = num / jnp.maximum(jnp.abs(den), einv_col)
            gate = _sigmoid(mo_ref[bb, :, h * D_V_M:(h + 1) * D_V_M])
            hm_ref[bb, :, h * D_V_M:(h + 1) * D_V_M] = (_rms(hh, gm) * gate).astype(BF16)
            vs = (v.astype(F32) * ws_col).astype(BF16)
            ws_blk = jnp.where(lane_l == 0, ws_col, 0.0).astype(BF16)
            ut = _dot(kT, jnp.concatenate([vs, ws_blk], axis=1))
            c_ref[bb, h] = wc[h:h + 1, 0:1] * cst + c2[h:h + 1, 0:1] * ut


def _mlstm_prompt(mq, kT, mv, mo, gl, gm, c0, m0):
    n_batch, t_len, _ = mq.shape
    n_chunks = t_len // CHUNK
    full = lambda a: pl.BlockSpec(a.shape, lambda c: (0,) * a.ndim)
    return pl.pallas_call(
        functools.partial(_mlstm_prompt_kernel, n_batch=n_batch),
        grid=(n_chunks,),
        in_specs=[
            pl.BlockSpec((n_batch, CHUNK, 256), lambda c: (0, c, 0)),
            pl.BlockSpec((n_batch, 256, CHUNK), lambda c: (0, 0, c)),
            pl.BlockSpec((n_batch, CHUNK, 512), lambda c: (0, c, 0)),
            pl.BlockSpec((n_batch, CHUNK, 512), lambda c: (0, c, 0)),
            pl.BlockSpec((n_batch, 8, CHUNK), lambda c: (0, 0, c)),
            full(gm), full(c0), full(m0),
        ],
        out_specs=(
            pl.BlockSpec((n_batch, CHUNK, 512), lambda c: (0, c, 0)),
            full(c0), full(m0),
        ),
        out_shape=(
            jax.ShapeDtypeStruct((n_batch, t_len, 512), BF16),
            jax.ShapeDtypeStruct(c0.shape, F32),
            jax.ShapeDtypeStruct(m0.shape, F32),
        ),
        compiler_params=_cparams(1),
        name="mlstm_prompt",
    )(mq, kT, mv, mo, gl, gm, c0, m0)


def _lambda_value(lam_ref):
    lp = lam_ref[...]
    s1 = jnp.sum(lp[0:1] * lp[1:2], axis=1, keepdims=True)
    s2 = jnp.sum(lp[2:3] * lp[3:4], axis=1, keepdims=True)
    return jnp.exp(s1) - jnp.exp(s2) + LAM_INIT


def _attn_prompt_kernel(q_ref, k_ref, v_ref, km_ref, vm_ref, lam_ref, gda_ref, o_ref, *, tq, tk):
    i = pl.program_id(2)
    q = q_ref[...]
    lane = lax.broadcasted_iota(jnp.int32, q.shape, 1)
    zero = jnp.zeros_like(q)
    qz = jnp.concatenate([jnp.where(lane < D_QK_DA, q, zero), jnp.where(lane >= D_QK_DA, q, zero)], axis=0)

    s = _dot_nt(qz, km_ref[...])
    m = jnp.max(s, axis=1, keepdims=True)
    p = jnp.exp(s - m)
    l = jnp.sum(p, axis=1, keepdims=True)
    acc = _dot(p.astype(BF16), vm_ref[...])

    def block(j, carry, masked):
        m, l, acc = carry
        start = pl.multiple_of(j * tk, tk)
        kb = k_ref[pl.ds(start, tk), :]
        vb = v_ref[pl.ds(start, tk), :]
        s = _dot_nt(qz, kb)
        if masked:
            row = lax.broadcasted_iota(jnp.int32, s.shape, 0)
            row = jnp.where(row >= tq, row - tq, row) + i * tq
            col = lax.broadcasted_iota(jnp.int32, s.shape, 1) + j * tk
            s = jnp.where(col <= row, s, -jnp.inf)
        m_new = jnp.maximum(m, jnp.max(s, axis=1, keepdims=True))
        alpha = jnp.exp(m - m_new)
        p = jnp.exp(s - m_new)
        l = alpha * l + jnp.sum(p, axis=1, keepdims=True)
        acc = alpha * acc + _dot(p.astype(BF16), vb)
        return m_new, l, acc

    n_full = (i * tq) // tk
    carry = lax.fori_loop(0, n_full, lambda j, c: block(j, c, False), (m, l, acc))
    n_diag = -(-tq // tk)
    for d in range(n_diag):
        carry = block(n_full + d, carry, True)
    m, l, acc = carry

    o = acc / l
    a = o[0:tq] - _lambda_value(lam_ref) * o[tq:2 * tq]
    o_ref[...] = (_rms(a, gda_ref[...]) * (1.0 - LAM_INIT)).astype(BF16)


def _attn_prompt(dq, dkb, dvb, km, vm, lam_p, gda, tq, tk):
    n_batch, t_len, _ = dq.shape
    full = lambda a: pl.BlockSpec(a.shape, lambda b, h, i: (0,) * a.ndim)
    return pl.pallas_call(
        functools.partial(_attn_prompt_kernel, tq=tq, tk=tk),
        grid=(n_batch, H_DA, t_len // tq),
        in_specs=[
            pl.BlockSpec((None, tq, 128), lambda b, h, i: (b, i, h)),
            pl.BlockSpec((None, t_len, 128), lambda b, h, i: (b, 0, h)),
            pl.BlockSpec((None, t_len, 128), lambda b, h, i: (b, 0, h)),
            pl.BlockSpec((N_META, 128), lambda b, h, i: (0, h)),
            pl.BlockSpec((N_META, 128), lambda b, h, i: (0, h)),
            full(lam_p), full(gda),
        ],
        out_specs=pl.BlockSpec((None, tq, 128), lambda b, h, i: (b, i, h)),
        out_shape=jax.ShapeDtypeStruct((n_batch, t_len, 512), BF16),
        compiler_params=_cparams(3),
        name="attn_prompt",
    )(dq, dkb, dvb, km, vm, lam_p, gda)


def _mlstm_sample_kernel(mq_ref, mk_ref, mv_ref, mo_ref, gl_ref, gm_ref, c_ref, n_ref, m_ref,
                         hm_ref, co_ref, no_ref, mo_out_ref, *, tb):
    gl = gl_ref[...]
    m_all = m_ref[...]
    gm = gm_ref[...]
    row8 = lax.broadcasted_iota(jnp.int32, (tb, D_V_M), 0)
    for h in range(H_M):
        q = mq_ref[:, h * D_QK_M:(h + 1) * D_QK_M]
        k = mk_ref[:, h * D_QK_M:(h + 1) * D_QK_M]
        v = mv_ref[:, h * D_V_M:(h + 1) * D_V_M]
        ig = gl[:, h:h + 1]
        lf = gl[:, H_M + h:H_M + h + 1]
        m = m_all[:, h:h + 1]
        n = n_ref[:, h, :]
        m_row = jnp.maximum(lf + m, ig)
        w_intra = jnp.exp(ig - m_row)
        w_inter = jnp.exp(lf + m - m_row)
        s = jnp.sum(q * k, axis=1, keepdims=True) * w_intra
        nq = jnp.sum(n * q, axis=1, keepdims=True)
        qb = q.astype(BF16)
        cq = jnp.zeros((tb, D_V_M), F32)
        for j in range(tb):
            res = _dot_nt(qb, c_ref[j, h].astype(BF16))
            cq = jnp.where(row8 == j, res, cq)
        num = s * v + w_inter * cq
        den = s + w_inter * nq
        hh = num / jnp.maximum(jnp.abs(den), jnp.exp(-m_row))
        gate = _sigmoid(mo_ref[:, h * D_V_M:(h + 1) * D_V_M])
        hm_ref[:, h * D_V_M:(h + 1) * D_V_M] = (_rms(hh, gm) * gate).astype(BF16)
        m_new = jnp.maximum(m + lf, ig)
        ws = jnp.exp(ig - m_new)
        wc = jnp.exp(m + lf - m_new)
        mo_out_ref[:, h:h + 1] = m_new
        no_ref[:, h, :] = wc * n + ws * k
        vT = jnp.concatenate([v * ws, jnp.zeros((D_V_M - tb, D_V_M), F32)], axis=0).T
        for j in range(tb):
            co_ref[j, h] = wc[j:j + 1, :] * c_ref[j, h] + vT[:, j:j + 1] * k[j:j + 1, :]


def _mlstm_sample(mq, mk, mv, mo, gl, gm, c, n, m, tb=8):
    nb = mq.shape[0]
    rows = lambda w: pl.BlockSpec((tb, w), lambda i: (i, 0))
    return pl.pallas_call(
        functools.partial(_mlstm_sample_kernel, tb=tb),
        grid=(nb // tb,),
        in_specs=[rows(256), rows(256), rows(512), rows(512), rows(128),
                  pl.BlockSpec(gm.shape, lambda i: (0, 0)),
                  pl.BlockSpec((tb, H_M, D_V_M, D_QK_M), lambda i: (i, 0, 0, 0)),
                  pl.BlockSpec((tb, H_M, D_QK_M), lambda i: (i, 0, 0)),
                  pl.BlockSpec((tb, H_M), lambda i: (i, 0))],
        out_specs=(rows(512),
                   pl.BlockSpec((tb, H_M, D_V_M, D_QK_M), lambda i: (i, 0, 0, 0)),
                   pl.BlockSpec((tb, H_M, D_QK_M), lambda i: (i, 0, 0)),
                   pl.BlockSpec((tb, H_M), lambda i: (i, 0))),
        out_shape=(jax.ShapeDtypeStruct((nb, 512), BF16),
                   jax.ShapeDtypeStruct(c.shape, F32),
                   jax.ShapeDtypeStruct(n.shape, F32),
                   jax.ShapeDtypeStruct(m.shape, F32)),
        compiler_params=_cparams(1),
        name="mlstm_sample",
    )(mq, mk, mv, mo, gl, gm, c, n, m)


_QROWS = 16


def _attn_decode_kernel(pt_ref, q_ref, ks_ref, vs_ref, lam_ref, gda_ref, *refs, n_pages):
    k_refs = refs[:n_pages]
    v_refs = refs[n_pages:2 * n_pages]
    o_ref = refs[2 * n_pages]
    row16 = lax.broadcasted_iota(jnp.int32, (_QROWS, 128), 0)
    lane16 = lax.broadcasted_iota(jnp.int32, (_QROWS, 128), 1)

    def per_map_rows(x):
        out = jnp.zeros((_QROWS, 128), F32)
        for h in range(H_DA):
            out = jnp.where(row16 // 2 == h, jnp.broadcast_to(x[h:h + 1], (_QROWS, 128)), out)
        return out

    own_map = (lane16 < D_QK_DA) == (row16 % 2 == 0)
    qz = jnp.where(own_map, per_map_rows(q_ref[...]), 0.0).astype(BF16)
    ks16 = per_map_rows(ks_ref[...].astype(BF16).astype(F32))
    vs16 = per_map_rows(vs_ref[...])
    s_self = jnp.sum(qz.astype(F32) * ks16, axis=1, keepdims=True)

    n_rows = k_refs[0].shape[0]
    rowi = lax.broadcasted_iota(jnp.int32, (_QROWS, n_rows), 0)
    coli = lax.broadcasted_iota(jnp.int32, (_QROWS, n_rows), 1)
    same_head = (coli % H_DA) == (rowi // 2)
    scores = [jnp.where(same_head, _dot_nt(qz, k_refs[j][...].astype(BF16)), -jnp.inf) for j in range(n_pages)]
    m = s_self
    for sj in scores:
        m = jnp.maximum(m, jnp.max(sj, axis=1, keepdims=True))
    p_self = jnp.exp(s_self - m)
    l = p_self
    acc = p_self * vs16
    for j in range(n_pages):
        p = jnp.exp(scores[j] - m)
        l = l + jnp.sum(p, axis=1, keepdims=True)
        acc = acc + _dot(p.astype(BF16), v_refs[j][...].astype(BF16))
    o = acc / l
    lam = _lambda_value(lam_ref)
    g = gda_ref[...]
    for h in range(H_DA):
        a = o[2 * h:2 * h + 1] - lam * o[2 * h + 1:2 * h + 2]
        o_ref[:, h * D_V_DA:(h + 1) * D_V_DA] = (_rms(a, g) * (1.0 - LAM_INIT)).astype(BF16)


def _attn_decode(page_table, q, ks, vs, lam_p, gda, cache_k, cache_v):
    nb = q.shape[0]
    n_pages = page_table.shape[0] // nb
    rows = cache_k.shape[1]
    tok = pl.BlockSpec((None, H_DA, 128), lambda b, pt: (b, 0, 0))
    full = lambda a: pl.BlockSpec(a.shape, lambda b, pt: (0,) * a.ndim)

    def page_spec(j):
        return pl.BlockSpec((None, rows, 128), lambda b, pt: (pt[b * n_pages + j], 0, 0))

    grid_spec = pltpu.PrefetchScalarGridSpec(
        num_scalar_prefetch=1,
        grid=(nb,),
        in_specs=[tok, tok, tok, full(lam_p), full(gda)]
                 + [page_spec(j) for j in range(n_pages)] + [page_spec(j) for j in range(n_pages)],
        out_specs=pl.BlockSpec((None, 1, 512), lambda b, pt: (b, 0, 0)),
    )
    return pl.pallas_call(
        functools.partial(_attn_decode_kernel, n_pages=n_pages),
        grid_spec=grid_spec,
        out_shape=jax.ShapeDtypeStruct((nb, 1, 512), BF16),
        compiler_params=_cparams(1),
        name="attn_decode",
    )(page_table, q, ks, vs, lam_p, gda, *([cache_k] * n_pages), *([cache_v] * n_pages))


_R_E1, _R_E2, _R_RANK1, _R_RANK2, _R_P1, _R_P2 = range(6)


def _route(logits, base):
    lane = lax.broadcasted_iota(jnp.int32, logits.shape, 1)
    big = jnp.int32(ROUTER_LANES)
    neg = -jnp.inf
    gl = jnp.where(lane < N_GROUPS, logits, neg)
    gmax = jnp.max(gl, axis=1, keepdims=True)
    g_idx = jnp.min(jnp.where(gl == gmax, lane, big), axis=1, keepdims=True)
    g_p = 1.0 / jnp.sum(jnp.exp(gl - gmax), axis=1, keepdims=True)
    lo = N_GROUPS + EXPERTS_PER_GROUP * g_idx
    in_group = (lane >= lo) & (lane < lo + EXPERTS_PER_GROUP)
    el = jnp.where(in_group, logits, neg)
    v1 = jnp.max(el, axis=1, keepdims=True)
    i1 = jnp.min(jnp.where(el == v1, lane, big), axis=1, keepdims=True)
    el2 = jnp.where(lane == i1, neg, el)
    v2 = jnp.max(el2, axis=1, keepdims=True)
    i2 = jnp.min(jnp.where(el2 == v2, lane, big), axis=1, keepdims=True)
    e2 = jnp.exp(v2 - v1)
    p1 = g_p / (1.0 + e2)
    p2 = g_p * e2 / (1.0 + e2)
    onehot = jnp.where(lane == i1, 1.0, jnp.where(lane == i2, 1.0, 0.0))
    tm = logits.shape[0]
    earlier = lax.broadcasted_iota(jnp.int32, (tm, tm), 1) < lax.broadcasted_iota(jnp.int32, (tm, tm), 0)
    ranks = _dot(jnp.where(earlier, 1.0, 0.0).astype(BF16), onehot.astype(BF16)) + base
    r1 = jnp.sum(jnp.where(lane == i1, ranks, 0.0), axis=1, keepdims=True)
    r2 = jnp.sum(jnp.where(lane == i2, ranks, 0.0), axis=1, keepdims=True)
    counts = base + jnp.sum(onehot, axis=0, keepdims=True)
    record = jnp.zeros(logits.shape, F32)
    for ln, val in ((_R_E1, (i1 - N_GROUPS).astype(F32)), (_R_E2, (i2 - N_GROUPS).astype(F32)),
                    (_R_RANK1, r1), (_R_RANK2, r2), (_R_P1, p1), (_R_P2, p2)):
        record = jnp.where(lane == ln, val, record)
    return record, counts


def _mixout_kernel(hm_ref, ha_ref, x_ref, wo_ref, gf_ref, wr_ref, br_ref, cnt_in_ref,
                   xmid_ref, h2_ref, route_ref, cnt_ref):
    @pl.when(pl.program_id(0) == 0)
    def _():
        cnt_ref[...] = cnt_in_ref[...]

    half = hm_ref.shape[1]
    y = _dot(hm_ref[...], wo_ref[0:half, :]) + _dot(ha_ref[...], wo_ref[half:2 * half, :])
    xmid = x_ref[...] + y
    xmid_ref[...] = xmid
    h2 = _rms(xmid, gf_ref[...])
    h2_ref[...] = h2
    record, counts = _route(_dot(h2.astype(BF16), wr_ref[...]) + br_ref[...], cnt_ref[...])
    route_ref[...] = record
    cnt_ref[...] = counts


def _mixout(hm, ha, x, wo, gf, wr, br, cnt_in, tm):
    rows, d = x.shape
    rs = lambda c: pl.BlockSpec((tm, c), lambda i: (i, 0))
    full = lambda a: pl.BlockSpec(a.shape, lambda i: (0,) * a.ndim)
    return pl.pallas_call(
        _mixout_kernel,
        grid=(rows // tm,),
        in_specs=[rs(512), rs(512), rs(d), full(wo), full(gf), full(wr), full(br), full(cnt_in)],
        out_specs=(rs(d), rs(d), rs(ROUTER_LANES), full(cnt_in)),
        out_shape=(jax.ShapeDtypeStruct((rows, d), F32),
                   jax.ShapeDtypeStruct((rows, d), F32),
                   jax.ShapeDtypeStruct((rows, ROUTER_LANES), F32),
                   jax.ShapeDtypeStruct(cnt_in.shape, F32)),
        compiler_params=_cparams(1),
        name="mixout",
    )(hm, ha, x, wo, gf, wr, br, cnt_in)


_TM_E = 256


def _row_copy(src_hbm, dst_hbm, src_row, dst_row, sem):
    return pltpu.make_async_copy(src_hbm.at[pl.ds(src_row, 1)], dst_hbm.at[pl.ds(dst_row, 1)], sem)


def _dispatch_kernel(fill_ref, pos_ref, h2_hbm, *refs, tm, zero_fill):
    if zero_fill:
        xs_hbm, zero_ref, sem = refs
    else:
        _, xs_hbm, sem = refs
    i = pl.program_id(0)
    n = pl.num_programs(0)

    if zero_fill:
        @pl.when(i == 0)
        def _():
            zero_ref[...] = jnp.zeros_like(zero_ref)
            fills = [pltpu.make_async_copy(zero_ref, xs_hbm.at[pl.ds(pl.multiple_of(fill_ref[e], 8), _TM_E)],
                                           sem.at[1]) for e in range(N_EXPERTS)]
            for f in fills:
                f.start()
            for f in fills:
                f.wait()

    base = i * tm

    def issue(t, carry):
        for c in range(2):
            _row_copy(h2_hbm, xs_hbm, base + t, pos_ref[0, 2 * t + c], sem.at[0]).start(priority=c)
        return carry

    lax.fori_loop(0, tm, issue, 0, unroll=8)

    def drain(t, carry):
        for c in range(2):
            _row_copy(h2_hbm, xs_hbm, 0, 0, sem.at[0]).wait()
        return carry

    @pl.when(i > 0)
    def _():
        lax.fori_loop(0, tm, drain, 0, unroll=8)

    @pl.when(i == n - 1)
    def _():
        lax.fori_loop(0, tm, drain, 0, unroll=8)


def _dispatch(fill_start, pos, h2, xs, tm, slot_rows):
    rows, d = h2.shape
    zero_fill = xs is None
    any_spec = pl.BlockSpec(memory_space=pl.ANY)
    in_specs = [pl.BlockSpec((None, 1, 2 * tm), lambda i, fs: (i, 0, 0), memory_space=pltpu.SMEM), any_spec]
    args = [fill_start, pos, h2]
    if not zero_fill:
        in_specs.append(any_spec)
        args.append(xs)
    grid_spec = pltpu.PrefetchScalarGridSpec(
        num_scalar_prefetch=1,
        grid=(rows // tm,),
        in_specs=in_specs,
        out_specs=any_spec,
        scratch_shapes=([pltpu.VMEM((_TM_E, d), F32)] if zero_fill else []) + [pltpu.SemaphoreType.DMA((2,))],
    )
    return pl.pallas_call(
        functools.partial(_dispatch_kernel, tm=tm, zero_fill=zero_fill),
        grid_spec=grid_spec,
        out_shape=jax.ShapeDtypeStruct((slot_rows, d), F32),
        input_output_aliases={} if zero_fill else {3: 0},
        compiler_params=_cparams(1),
        name="moe_dispatch",
    )(*args)


def _expert_kernel(te_ref, nu_ref, x_ref, wg_ref, wu_ref, wd_ref, o_ref, wg_s, wu_s, wd_s):
    i = pl.program_id(0)

    @pl.when(i < nu_ref[0])
    def _():
        @pl.when((i == 0) | (te_ref[i] != te_ref[jnp.maximum(i - 1, 0)]))
        def _():
            wg_s[...] = wg_ref[...].astype(BF16)
            wu_s[...] = wu_ref[...].astype(BF16)
            wd_s[...] = wd_ref[...].astype(BF16)

        x = x_ref[...].astype(BF16)
        g = _dot(x, wg_s[...])
        u = _dot(x, wu_s[...])
        act = (g * _sigmoid(g) * u).astype(BF16)
        o_ref[...] = _dot(act, wd_s[...])


def _experts(tile_expert, n_used, xs, w_gate, w_up, w_down, n_tiles):
    _, d = xs.shape
    n_e, _, d_e = w_gate.shape
    used = lambda i, te, nu: (jnp.minimum(i, nu[0] - 1), 0)
    grid_spec = pltpu.PrefetchScalarGridSpec(
        num_scalar_prefetch=2,
        grid=(n_tiles,),
        in_specs=[pl.BlockSpec((_TM_E, d), used),
                  pl.BlockSpec((None, d, d_e), lambda i, te, nu: (te[i], 0, 0)),
                  pl.BlockSpec((None, d, d_e), lambda i, te, nu: (te[i], 0, 0)),
                  pl.BlockSpec((None, d_e, d), lambda i, te, nu: (te[i], 0, 0))],
        out_specs=pl.BlockSpec((_TM_E, d), used),
        scratch_shapes=[pltpu.VMEM((d, d_e), BF16), pltpu.VMEM((d, d_e), BF16), pltpu.VMEM((d_e, d), BF16)],
    )
    return pl.pallas_call(
        _expert_kernel,
        grid_spec=grid_spec,
        out_shape=jax.ShapeDtypeStruct(xs.shape, F32),
        compiler_params=_cparams(1),
        name="moe_experts",
    )(tile_expert, n_used, xs, w_gate, w_up, w_down)


def _combine_kernel(pos_ref, posn_ref, route_ref, xmid_ref, gfin_ref, o_hbm, y_ref, buf, sem, *, tm):
    i = pl.program_id(0)
    n = pl.num_programs(0)

    def start_gather(p_ref, slot):
        def body(t, carry):
            for c in range(2):
                pltpu.make_async_copy(o_hbm.at[pl.ds(p_ref[0, 2 * t + c], 1)], buf.at[slot, c, pl.ds(t, 1)],
                                      sem.at[slot]).start(priority=c)
            return carry
        lax.fori_loop(0, tm, body, 0, unroll=8)

    def finish(slot):
        def body(t, carry):
            for c in range(2):
                pltpu.make_async_copy(o_hbm.at[pl.ds(0, 1)], buf.at[slot, c, pl.ds(0, 1)], sem.at[slot]).wait()
            return carry
        lax.fori_loop(0, tm, body, 0, unroll=8)
        route = route_ref[...]
        lane = lax.broadcasted_iota(jnp.int32, route.shape, 1)
        p1 = jnp.sum(jnp.where(lane == _R_P1, route, 0.0), axis=1, keepdims=True)
        p2 = jnp.sum(jnp.where(lane == _R_P2, route, 0.0), axis=1, keepdims=True)
        y = xmid_ref[...] + p1 * buf[slot, 0] + p2 * buf[slot, 1]
        y_ref[...] = _rms(y, gfin_ref[...])

    @pl.when(i == 0)
    def _():
        start_gather(pos_ref, 0)

    for slot in range(2):
        @pl.when(i % 2 == slot)
        def _(slot=slot):
            @pl.when(i + 1 < n)
            def _():
                start_gather(posn_ref, 1 - slot)
            finish(slot)


def _combine(pos, route, xmid, gfin, o, tm):
    rows, d = xmid.shape
    n = rows // tm
    grid_spec = pl.GridSpec(
        grid=(n,),
        in_specs=[pl.BlockSpec((None, 1, 2 * tm), lambda i: (i, 0, 0), memory_space=pltpu.SMEM),
                  pl.BlockSpec((None, 1, 2 * tm), lambda i: (jnp.minimum(i + 1, n - 1), 0, 0),
                               memory_space=pltpu.SMEM),
                  pl.BlockSpec((tm, ROUTER_LANES), lambda i: (i, 0)),
                  pl.BlockSpec((tm, d), lambda i: (i, 0)),
                  pl.BlockSpec(gfin.shape, lambda i: (0, 0)),
                  pl.BlockSpec(memory_space=pl.ANY)],
        out_specs=pl.BlockSpec((tm, d), lambda i: (i, 0)),
        scratch_shapes=[pltpu.VMEM((2, 2, tm, d), F32), pltpu.SemaphoreType.DMA((2,))],
    )
    return pl.pallas_call(
        functools.partial(_combine_kernel, tm=tm),
        grid_spec=grid_spec,
        out_shape=jax.ShapeDtypeStruct((rows, d), F32),
        compiler_params=_cparams(1),
        name="moe_combine",
    )(pos, pos, route, xmid, gfin, o)


def kernel(x_prompt, x_sample, cache_k, cache_v, state_C, state_n, state_m, page_table, meta_tokens, norm_mix, w_in, b_gates, head_norm_m, lambda_q1, lambda_k1, lambda_q2, lambda_k2, head_norm_da, w_out, norm_ffn, w_group, b_group, w_router, b_router, w_gate_e, w_up_e, w_down_e, norm_final):
    n_batch, seq, d_model = x_prompt.shape
    nb_dec = x_sample.shape[0]
    assert cache_k.shape[0] == 1 and x_sample.shape[1] == 1, "one layer, one new token per sequence"
    n_pages = page_table.shape[1]

    w = w_in[0]
    cuts = [0]
    for wd_ in (256, 256, 512, 512, 4, 4, 512, 512, 512):
        cuts.append(cuts[-1] + wd_)
    w_mq, w_mk, w_mv, w_mo, w_mi, w_mf, w_dq, w_dk, w_dv = (w[:, cuts[i]:cuts[i + 1]] for i in range(9))
    w_gates = jnp.concatenate([w_mi, w_mf], axis=1)
    w_p = jnp.concatenate([w_mq, w_mv, w_mo, w_dq, w_dk, w_dv], axis=1).astype(BF16)
    wkT = w_mk.T.astype(BF16)
    wgT = jnp.pad(w_gates.T, ((0, 8), (0, 0))).astype(BF16)
    w_s = jnp.concatenate([w_mq, w_mk, w_mv, w_mo, w_dq, w_dk, w_dv,
                           jnp.pad(w_gates, ((0, 0), (0, 120)))], axis=1).astype(BF16)
    bg_col = b_gates[0].reshape(8, 1).astype(F32)
    bg_row = jnp.pad(b_gates[0].reshape(1, 8), ((0, 0), (0, 120))).astype(F32)
    g_mix = norm_mix[0].reshape(1, d_model)
    gm = head_norm_m[0].reshape(1, D_V_M)
    gda = head_norm_da[0].reshape(1, D_V_DA)
    lam_p = jnp.stack([lambda_q1[0], lambda_k1[0], lambda_q2[0], lambda_k2[0]]).astype(F32)
    wo = w_out[0].astype(BF16)
    g_ffn = norm_ffn[0].reshape(1, d_model)
    w_r = jnp.pad(jnp.concatenate([w_group[0], w_router[0]], axis=1),
                  ((0, 0), (0, ROUTER_LANES - N_GROUPS - N_EXPERTS))).astype(BF16)
    b_r = jnp.pad(jnp.concatenate([b_group[0], b_router[0]]).reshape(1, -1),
                  ((0, 0), (0, ROUTER_LANES - N_GROUPS - N_EXPERTS))).astype(F32)
    g_fin = norm_final.reshape(1, d_model)

    xp = x_prompt.reshape(n_batch * seq, d_model)
    (mq, mv, mo, dq, dk, dv, dkb, dvb, kT, gl) = _inproj_prompt(xp, g_mix, w_p, wkT, wgT, bg_col, n_batch, seq, 512)
    x_meta = jnp.pad(meta_tokens.astype(F32), ((0, CHUNK - N_META), (0, 0)))
    (mq_m, mv_m, mo_m, _, dk_m, dv_m, dkb_m, dvb_m, kT_m, gl_m) = _inproj_prompt(
        x_meta, g_mix, w_p, wkT, wgT, bg_col, 1, CHUNK, CHUNK)
    (mq_s, mk_s, mv_s, mo_s, dq_s, dk_s, dv_s, gl_s) = _inproj_sample(
        x_sample.reshape(nb_dec, d_model), g_mix, w_s, bg_row)

    lane = jnp.arange(CHUNK)[None, None, :]
    pad_gate = jnp.where(jnp.arange(8)[None, :, None] < H_M, PAD_LOG_GATE, 0.0)
    gl_m = jnp.where(lane < N_META, gl_m, pad_gate)
    c_zero = jnp.zeros((1, H_M, D_QK_M, 256), F32)
    m_zero = jnp.zeros((1, H_M, CHUNK), F32)
    _, c_meta, m_meta = _mlstm_prompt(mq_m.reshape(1, CHUNK, 256), kT_m, mv_m.reshape(1, CHUNK, 512),
                                      mo_m.reshape(1, CHUNK, 512), gl_m, gm, c_zero, m_zero)
    hm_p, c_fin, m_fin = _mlstm_prompt(
        mq.reshape(n_batch, seq, 256), kT, mv.reshape(n_batch, seq, 512), mo.reshape(n_batch, seq, 512), gl, gm,
        jnp.broadcast_to(c_meta, (n_batch,) + c_meta.shape[1:]),
        jnp.broadcast_to(m_meta, (n_batch,) + m_meta.shape[1:]))
    hm_s, c_s, n_s, m_s = _mlstm_sample(mq_s, mk_s, mv_s, mo_s, gl_s, gm,
                                        state_C[0].astype(F32), state_n[0].astype(F32), state_m[0].astype(F32))

    ha_p = _attn_prompt(dq.reshape(n_batch, seq, 512), dkb.reshape(n_batch, seq, 512),
                        dvb.reshape(n_batch, seq, 512), dkb_m[:N_META], dvb_m[:N_META], lam_p, gda, 256, 512)
    n_pool, page = cache_k.shape[1], cache_k.shape[2]
    ha_s = _attn_decode(page_table.reshape(-1).astype(jnp.int32),
                        dq_s.reshape(nb_dec, H_DA, 128), dk_s.reshape(nb_dec, H_DA, 128),
                        dv_s.reshape(nb_dec, H_DA, 128), lam_p, gda,
                        cache_k.reshape(n_pool, page * H_DA, 128), cache_v.reshape(n_pool, page * H_DA, 128))

    n_prompt_rows = n_batch * seq
    cnt0 = jnp.zeros((1, ROUTER_LANES), F32)
    xmid_p, h2_p, route_p, cnt_p = _mixout(hm_p.reshape(n_prompt_rows, 512), ha_p.reshape(n_prompt_rows, 512), xp,
                                           wo, g_ffn, w_r, b_r, cnt0, 512)
    xmid_s, h2_s, route_s, cnt_all = _mixout(hm_s, ha_s.reshape(nb_dec, 512), x_sample.reshape(nb_dec, d_model),
                                             wo, g_ffn, w_r, b_r, cnt_p, nb_dec)

    n_tiles = (2 * (n_prompt_rows + nb_dec)) // _TM_E + N_EXPERTS
    slot_rows = (n_tiles + 1) * _TM_E
    counts = cnt_all[0, N_GROUPS:N_GROUPS + N_EXPERTS].astype(jnp.int32)
    padded = ((counts + _TM_E - 1) // _TM_E) * _TM_E
    ends = jnp.cumsum(padded)
    starts = ends - padded
    n_used = (ends[-1:] // _TM_E).astype(jnp.int32)
    tile_first_row = jnp.arange(n_tiles, dtype=jnp.int32) * _TM_E
    tile_expert = jnp.minimum(jnp.sum(tile_first_row[:, None] >= ends[None, :], axis=1), N_EXPERTS - 1)
    fill_start = ((starts + counts) // 8 * 8).astype(jnp.int32)

    def slots(route, tm):
        e = route[:, _R_E1:_R_E2 + 1].astype(jnp.int32)
        r = route[:, _R_RANK1:_R_RANK2 + 1].astype(jnp.int32)
        return (starts[e] + r).astype(jnp.int32).reshape(route.shape[0] // tm, 1, 2 * tm)

    tm_c = 256
    pos_p = slots(route_p, tm_c)
    pos_s = slots(route_s, nb_dec)
    xs = _dispatch(fill_start, pos_p, h2_p, None, tm_c, slot_rows)
    xs = _dispatch(fill_start, pos_s, h2_s, xs, nb_dec, slot_rows)
    eo = _experts(tile_expert.astype(jnp.int32), n_used, xs, w_gate_e[0], w_up_e[0], w_down_e[0], n_tiles)
    y_p = _combine(pos_p, route_p, xmid_p, g_fin, eo, tm_c)
    y_s = _combine(pos_s, route_s, xmid_s, g_fin, eo, nb_dec)

    def with_meta(meta_rows, real_rows):
        meta_b = jnp.broadcast_to(meta_rows[None, :N_META * H_DA], (n_batch, N_META * H_DA, 128))
        full = jnp.concatenate([meta_b, real_rows.reshape(n_batch, seq * H_DA, 128)], axis=1)
        return full.reshape(1, n_batch, N_META + seq, H_DA, 128)

    k_prompt = with_meta(dk_m, dk).astype(cache_k.dtype)
    v_prompt = with_meta(dv_m, dv).astype(cache_v.dtype)
    C_prompt = jnp.swapaxes(c_fin[:, :, :, :D_V_M], 2, 3)[None].astype(state_C.dtype)
    n_prompt = c_fin[:, :, :, D_V_M][None].astype(state_n.dtype)
    m_prompt = m_fin[:, :, 0][None].astype(state_m.dtype)
    return (y_p.reshape(n_batch, seq, d_model).astype(x_prompt.dtype),
            y_s.reshape(nb_dec, 1, d_model).astype(x_sample.dtype),
            k_prompt, v_prompt, C_prompt, n_prompt, m_prompt,
            dk_s.reshape(1, nb_dec, 1, H_DA, 128).astype(cache_k.dtype),
            dv_s.reshape(1, nb_dec, 1, H_DA, 128).astype(cache_v.dtype),
            c_s[None].astype(state_C.dtype), n_s[None].astype(state_n.dtype), m_s[None].astype(state_m.dtype))
```

```python
import functools
import math

import jax
import jax.numpy as jnp
from jax import lax
from jax.experimental import pallas as pl
from jax.experimental.pallas import tpu as pltpu

F32 = jnp.float32
BF16 = jnp.bfloat16

EPS = 1e-6
N_META = 16
H_M = 4
D_QK_M = 64
D_V_M = 128
H_DA = 4
D_QK_DA = 64
D_V_DA = 128
CHUNK = 128
N_GROUPS = 4
EXPERTS_PER_GROUP = 8
N_EXPERTS = N_GROUPS * EXPERTS_PER_GROUP
LAM_INIT = 0.8 - 0.6 * math.exp(-0.3 * 0)
PAD_LOG_GATE = -1e30
ROUTER_LANES = 128

VMEM_LIMIT = 56 * 1024 * 1024


def _cparams(n_axes, vmem=None):
    return pltpu.CompilerParams(dimension_semantics=("arbitrary",) * n_axes,
                                vmem_limit_bytes=vmem if vmem else VMEM_LIMIT)


def _dot(a, b):
    return jnp.dot(a, b, preferred_element_type=F32)


def _dot_nt(a, b):
    return lax.dot_general(a, b, (((1,), (1,)), ((), ())), preferred_element_type=F32)


def _rms(x, g):
    return x * lax.rsqrt(jnp.mean(x * x, axis=-1, keepdims=True) + EPS) * g


def _log_sigmoid(x):
    return jnp.minimum(x, 0.0) - jnp.log1p(jnp.exp(-jnp.abs(x)))


def _sigmoid(x):
    return 1.0 / (1.0 + jnp.exp(-x))


def _store_row_tiles(ref, x):
    rows, d = x.shape
    for j in range(d // 128):
        ref[pl.ds(j, rows, stride=d // 128), :] = x[:, j * 128:(j + 1) * 128]


def _load_row_tile_cols(ref, j, rows, n_chunks):
    return ref[pl.ds(j, rows, stride=n_chunks), :]


_P_MQ, _P_MV, _P_MO, _P_DQ, _P_DK, _P_DV = 0, 256, 768, 1280, 1792, 2304
_P_WIDTH = 2816


def _inproj_prompt_kernel(x_ref, g_ref, w_ref, wkT_ref, wgT_ref, bg_ref,
                          mq_ref, mv_ref, mo_ref, dq_ref, dk_ref, dv_ref, dkb_ref, dvb_ref, kT_ref, gl_ref):
    h = _rms(x_ref[...], g_ref[...]).astype(BF16)
    mq_ref[...] = _dot(h, w_ref[:, _P_MQ:_P_MQ + 256]).astype(BF16)
    mv_ref[...] = _dot(h, w_ref[:, _P_MV:_P_MV + 512]).astype(BF16)
    mo_ref[...] = _dot(h, w_ref[:, _P_MO:_P_MO + 512])
    dq_ref[...] = (_dot(h, w_ref[:, _P_DQ:_P_DQ + 512]) * (D_QK_DA ** -0.5)).astype(BF16)
    tm = x_ref.shape[0]
    dk = _dot(h, w_ref[:, _P_DK:_P_DK + 512])
    dkb_ref[...] = dk.astype(BF16)
    dv = _dot(h, w_ref[:, _P_DV:_P_DV + 512])
    dvb_ref[...] = dv.astype(BF16)
    for hd in range(H_DA):
        dk_ref[pl.ds(hd, tm, stride=H_DA), :] = dk[:, hd * 128:(hd + 1) * 128]
        dv_ref[pl.ds(hd, tm, stride=H_DA), :] = dv[:, hd * 128:(hd + 1) * 128]
    kT_ref[...] = (_dot_nt(wkT_ref[...], h) * (D_QK_M ** -0.5)).astype(BF16)
    gt = _dot_nt(wgT_ref[...], h)[0:8, :] + bg_ref[...]
    row = lax.broadcasted_iota(jnp.int32, gt.shape, 0)
    gl_ref[...] = jnp.where(row < H_M, gt, _log_sigmoid(gt))


def _inproj_prompt(x, g, w, wkT, wgT, bg, n_batch, t_len, tm):
    rows, d = x.shape
    tiles_per_b = t_len // tm
    row_spec = lambda c: pl.BlockSpec((tm, c), lambda i: (i, 0))
    full = lambda a: pl.BlockSpec(a.shape, lambda i: (0,) * a.ndim)
    t_spec = lambda c: pl.BlockSpec((None, c, tm), lambda i: (i // tiles_per_b, 0, i % tiles_per_b))
    out_shape = (
        jax.ShapeDtypeStruct((rows, 256), BF16),
        jax.ShapeDtypeStruct((rows, 512), BF16),
        jax.ShapeDtypeStruct((rows, 512), F32),
        jax.ShapeDtypeStruct((rows, 512), BF16),
        jax.ShapeDtypeStruct((rows * H_DA, 128), F32),
        jax.ShapeDtypeStruct((rows * H_DA, 128), F32),
        jax.ShapeDtypeStruct((rows, 512), BF16),
        jax.ShapeDtypeStruct((rows, 512), BF16),
        jax.ShapeDtypeStruct((n_batch, 256, t_len), BF16),
        jax.ShapeDtypeStruct((n_batch, 8, t_len), F32),
    )
    head_rows = pl.BlockSpec((tm * H_DA, 128), lambda i: (i, 0))
    out_specs = (row_spec(256), row_spec(512), row_spec(512), row_spec(512), head_rows, head_rows,
                 row_spec(512), row_spec(512), t_spec(256), t_spec(8))
    return pl.pallas_call(
        _inproj_prompt_kernel,
        grid=(rows // tm,),
        in_specs=[row_spec(d), full(g), full(w), full(wkT), full(wgT), full(bg)],
        out_specs=out_specs,
        out_shape=out_shape,
        compiler_params=_cparams(1),
        name="inproj_prompt",
    )(x, g, w, wkT, wgT, bg)


_S_MQ, _S_MK, _S_MV, _S_MO, _S_DQ, _S_DK, _S_DV, _S_G = 0, 256, 512, 1024, 1536, 2048, 2560, 3072
_S_WIDTH = 3200


def _inproj_sample_kernel(x_ref, g_ref, w_ref, bg_ref,
                          mq_ref, mk_ref, mv_ref, mo_ref, dq_ref, dk_ref, dv_ref, gl_ref):
    h = _rms(x_ref[...], g_ref[...]).astype(BF16)
    mq_ref[...] = _dot(h, w_ref[:, _S_MQ:_S_MQ + 256])
    mk_ref[...] = _dot(h, w_ref[:, _S_MK:_S_MK + 256]) * (D_QK_M ** -0.5)
    mv_ref[...] = _dot(h, w_ref[:, _S_MV:_S_MV + 512])
    mo_ref[...] = _dot(h, w_ref[:, _S_MO:_S_MO + 512])
    dq_ref[...] = _dot(h, w_ref[:, _S_DQ:_S_DQ + 512]) * (D_QK_DA ** -0.5)
    dk_ref[...] = _dot(h, w_ref[:, _S_DK:_S_DK + 512])
    dv_ref[...] = _dot(h, w_ref[:, _S_DV:_S_DV + 512])
    gt = _dot(h, w_ref[:, _S_G:_S_G + 128]) + bg_ref[...]
    lane = lax.broadcasted_iota(jnp.int32, gt.shape, 1)
    gl_ref[...] = jnp.where(lane < H_M, gt, _log_sigmoid(gt))


def _inproj_sample(x, g, w, bg):
    rows, d = x.shape
    full = lambda a: pl.BlockSpec(a.shape, lambda i: (0,) * a.ndim)
    widths = (256, 256, 512, 512, 512, 512, 512, 128)
    return pl.pallas_call(
        _inproj_sample_kernel,
        grid=(1,),
        in_specs=[full(x), full(g), full(w), full(bg)],
        out_specs=tuple(pl.BlockSpec((rows, c), lambda i: (0, 0)) for c in widths),
        out_shape=tuple(jax.ShapeDtypeStruct((rows, c), F32) for c in widths),
        compiler_params=_cparams(1),
        name="inproj_sample",
    )(x, g, w, bg)


def _scan_lanes(x, op, identity):
    lane = lax.broadcasted_iota(jnp.int32, x.shape, 1)
    k = 1
    while k < x.shape[1]:
        shifted = pltpu.roll(x, k, 1)
        x = op(x, jnp.where(lane >= k, shifted, identity))
        k *= 2
    return x


def _mlstm_prompt_kernel(mq_ref, kT_ref, mv_ref, mo_ref, gl_ref, gm_ref, c0_ref, m0_ref,
                         hm_ref, c_ref, m_ref, *, n_batch):
    step = pl.program_id(0)

    @pl.when(step == 0)
    def _():
        c_ref[...] = c0_ref[...]
        m_ref[...] = m0_ref[...]

    L = CHUNK
    t_idx = lax.broadcasted_iota(jnp.int32, (L, L), 0)
    s_idx = lax.broadcasted_iota(jnp.int32, (L, L), 1)
    causal = s_idx <= t_idx
    lane_l = lax.broadcasted_iota(jnp.int32, (L, L), 1)
    ones_blk = jnp.where(lane_l == 0, 1.0, 0.0).astype(BF16)
    gm = gm_ref[...]

    for bb in range(n_batch):
        gl = gl_ref[bb]
        ig = gl[0:H_M]
        lf = gl[H_M:2 * H_M]
        b = _scan_lanes(lf, jnp.add, 0.0)
        r = ig - b
        cm = _scan_lanes(r, jnp.maximum, -jnp.inf)
        a = b + cm
        m = m_ref[bb]
        m_row = jnp.maximum(b + m, a)
        u = b - m_row
        beta = jnp.exp(b + m - m_row)
        einv = jnp.exp(-m_row)
        b_last = jnp.broadcast_to(b[:, L - 1:L], b.shape)
        g = b_last + jnp.broadcast_to(cm[:, L - 1:L], b.shape)
        m_new = jnp.maximum(m + b_last, g)
        wc = jnp.exp(m + b_last - m_new)
        c2 = jnp.exp(g - m_new)
        ws = jnp.exp(b_last + r - g)
        m_ref[bb] = m_new
        pack = jnp.concatenate([u, beta, einv, ws, jnp.zeros((L - 4 * H_M, L), F32)], axis=0)
        packT = pack.T
        for h in range(H_M):
            q = mq_ref[bb, :, h * D_QK_M:(h + 1) * D_QK_M]
            kT = kT_ref[bb, h * D_QK_M:(h + 1) * D_QK_M, :]
            v = mv_ref[bb, :, h * D_V_M:(h + 1) * D_V_M]
            u_col = packT[:, h:h + 1]
            beta_col = packT[:, H_M + h:H_M + h + 1]
            einv_col = packT[:, 2 * H_M + h:2 * H_M + h + 1]
            ws_col = packT[:, 3 * H_M + h:3 * H_M + h + 1]
            w_intra = jnp.where(causal, jnp.exp(r[h:h + 1, :] + u_col), 0.0)
            s = (_dot(q, kT) * w_intra).astype(BF16)
            v_ext = jnp.concatenate([v, ones_blk], axis=1)
            cst = c_ref[bb, h]
            numden = _dot(s, v_ext) + beta_col * _dot(q, cst.astype(BF16))
            num = numden[:, 0:D_V_M]
            den = numden[:, D_V_M:D_V_M + 1]
            hh = num / jnp.maximum(jnp.abs(den), einv_col)
            gate = _sigmoid(mo_ref[bb, :, h * D_V_M:(h + 1) * D_V_M])
            hm_ref[bb, :, h * D_V_M:(h + 1) * D_V_M] = (_rms(hh, gm) * gate).astype(BF16)
            vs = (v.astype(F32) * ws_col).astype(BF16)
            ws_blk = jnp.where(lane_l == 0, ws_col, 0.0).astype(BF16)
            ut = _dot(kT, jnp.concatenate([vs, ws_blk], axis=1))
            c_ref[bb, h] = wc[h:h + 1, 0:1] * cst + c2[h:h + 1, 0:1] * ut


def _mlstm_prompt(mq, kT, mv, mo, gl, gm, c0, m0):
    n_batch, t_len, _ = mq.shape
    n_chunks = t_len // CHUNK
    full = lambda a: pl.BlockSpec(a.shape, lambda c: (0,) * a.ndim)
    return pl.pallas_call(
        functools.partial(_mlstm_prompt_kernel, n_batch=n_batch),
        grid=(n_chunks,),
        in_specs=[
            pl.BlockSpec((n_batch, CHUNK, 256), lambda c: (0, c, 0)),
            pl.BlockSpec((n_batch, 256, CHUNK), lambda c: (0, 0, c)),
            pl.BlockSpec((n_batch, CHUNK, 512), lambda c: (0, c, 0)),
            pl.BlockSpec((n_batch, CHUNK, 512), lambda c: (0, c, 0)),
            pl.BlockSpec((n_batch, 8, CHUNK), lambda c: (0, 0, c)),
            full(gm), full(c0), full(m0),
        ],
        out_specs=(
            pl.BlockSpec((n_batch, CHUNK, 512), lambda c: (0, c, 0)),
            full(c0), full(m0),
        ),
        out_shape=(
            jax.ShapeDtypeStruct((n_batch, t_len, 512), BF16),
            jax.ShapeDtypeStruct(c0.shape, F32),
            jax.ShapeDtypeStruct(m0.shape, F32),
        ),
        compiler_params=_cparams(1),
        name="mlstm_prompt",
    )(mq, kT, mv, mo, gl, gm, c0, m0)


def _lambda_value(lam_ref):
    lp = lam_ref[...]
    s1 = jnp.sum(lp[0:1] * lp[1:2], axis=1, keepdims=True)
    s2 = jnp.sum(lp[2:3] * lp[3:4], axis=1, keepdims=True)
    return jnp.exp(s1) - jnp.exp(s2) + LAM_INIT


def _attn_prompt_kernel(q_ref, k_ref, v_ref, km_ref, vm_ref, lam_ref, gda_ref, o_ref, *, tq, tk):
    i = pl.program_id(2)
    q = q_ref[...]
    lane = lax.broadcasted_iota(jnp.int32, q.shape, 1)
    zero = jnp.zeros_like(q)
    qz = jnp.concatenate([jnp.where(lane < D_QK_DA, q, zero), jnp.where(lane >= D_QK_DA, q, zero)], axis=0)

    s = _dot_nt(qz, km_ref[...])
    m = jnp.max(s, axis=1, keepdims=True)
    p = jnp.exp(s - m)
    l = jnp.sum(p, axis=1, keepdims=True)
    acc = _dot(p.astype(BF16), vm_ref[...])

    def block(j, carry, masked):
        m, l, acc = carry
        start = pl.multiple_of(j * tk, tk)
        kb = k_ref[pl.ds(start, tk), :]
        vb = v_ref[pl.ds(start, tk), :]
        s = _dot_nt(qz, kb)
        if masked:
            row = lax.broadcasted_iota(jnp.int32, s.shape, 0)
            row = jnp.where(row >= tq, row - tq, row) + i * tq
            col = lax.broadcasted_iota(jnp.int32, s.shape, 1) + j * tk
            s = jnp.where(col <= row, s, -jnp.inf)
        m_new = jnp.maximum(m, jnp.max(s, axis=1, keepdims=True))
        alpha = jnp.exp(m - m_new)
        p = jnp.exp(s - m_new)
        l = alpha * l + jnp.sum(p, axis=1, keepdims=True)
        acc = alpha * acc + _dot(p.astype(BF16), vb)
        return m_new, l, acc

    n_full = (i * tq) // tk
    carry = lax.fori_loop(0, n_full, lambda j, c: block(j, c, False), (m, l, acc))
    n_diag = -(-tq // tk)
    for d in range(n_diag):
        carry = block(n_full + d, carry, True)
    m, l, acc = carry

    o = acc / l
    a = o[0:tq] - _lambda_value(lam_ref) * o[tq:2 * tq]
    o_ref[...] = (_rms(a, gda_ref[...]) * (1.0 - LAM_INIT)).astype(BF16)


def _attn_prompt(dq, dkb, dvb, km, vm, lam_p, gda, tq, tk):
    n_batch, t_len, _ = dq.shape
    full = lambda a: pl.BlockSpec(a.shape, lambda b, h, i: (0,) * a.ndim)
    return pl.pallas_call(
        functools.partial(_attn_prompt_kernel, tq=tq, tk=tk),
        grid=(n_batch, H_DA, t_len // tq),
        in_specs=[
            pl.BlockSpec((None, tq, 128), lambda b, h, i: (b, i, h)),
            pl.BlockSpec((None, t_len, 128), lambda b, h, i: (b, 0, h)),
            pl.BlockSpec((None, t_len, 128), lambda b, h, i: (b, 0, h)),
            pl.BlockSpec((N_META, 128), lambda b, h, i: (0, h)),
            pl.BlockSpec((N_META, 128), lambda b, h, i: (0, h)),
            full(lam_p), full(gda),
        ],
        out_specs=pl.BlockSpec((None, tq, 128), lambda b, h, i: (b, i, h)),
        out_shape=jax.ShapeDtypeStruct((n_batch, t_len, 512), BF16),
        compiler_params=_cparams(3),
        name="attn_prompt",
    )(dq, dkb, dvb, km, vm, lam_p, gda)


def _mlstm_sample_kernel(mq_ref, mk_ref, mv_ref, mo_ref, gl_ref, gm_ref, c_ref, n_ref, m_ref,
                         hm_ref, co_ref, no_ref, mo_out_ref, *, tb):
    gl = gl_ref[...]
    m_all = m_ref[...]
    gm = gm_ref[...]
    row8 = lax.broadcasted_iota(jnp.int32, (tb, D_V_M), 0)
    for h in range(H_M):
        q = mq_ref[:, h * D_QK_M:(h + 1) * D_QK_M]
        k = mk_ref[:, h * D_QK_M:(h + 1) * D_QK_M]
        v = mv_ref[:, h * D_V_M:(h + 1) * D_V_M]
        ig = gl[:, h:h + 1]
        lf = gl[:, H_M + h:H_M + h + 1]
        m = m_all[:, h:h + 1]
        n = n_ref[:, h, :]
        m_row = jnp.maximum(lf + m, ig)
        w_intra = jnp.exp(ig - m_row)
        w_inter = jnp.exp(lf + m - m_row)
        s = jnp.sum(q * k, axis=1, keepdims=True) * w_intra
        nq = jnp.sum(n * q, axis=1, keepdims=True)
        qb = q.astype(BF16)
        cq = jnp.zeros((tb, D_V_M), F32)
        for j in range(tb):
            res = _dot_nt(qb, c_ref[j, h].astype(BF16))
            cq = jnp.where(row8 == j, res, cq)
        num = s * v + w_inter * cq
        den = s + w_inter * nq
        hh = num / jnp.maximum(jnp.abs(den), jnp.exp(-m_row))
        gate = _sigmoid(mo_ref[:, h * D_V_M:(h + 1) * D_V_M])
        hm_ref[:, h * D_V_M:(h + 1) * D_V_M] = (_rms(hh, gm) * gate).astype(BF16)
        m_new = jnp.maximum(m + lf, ig)
        ws = jnp.exp(ig - m_new)
        wc = jnp.exp(m + lf - m_new)
        mo_out_ref[:, h:h + 1] = m_new
        no_ref[:, h, :] = wc * n + ws * k
        vT = jnp.concatenate([v * ws, jnp.zeros((D_V_M - tb, D_V_M), F32)], axis=0).T
        for j in range(tb):
            co_ref[j, h] = wc[j:j + 1, :] * c_ref[j, h] + vT[:, j:j + 1] * k[j:j + 1, :]


def _mlstm_sample(mq, mk, mv, mo, gl, gm, c, n, m, tb=8):
    nb = mq.shape[0]
    rows = lambda w: pl.BlockSpec((tb, w), lambda i: (i, 0))
    return pl.pallas_call(
        functools.partial(_mlstm_sample_kernel, tb=tb),
        grid=(nb // tb,),
        in_specs=[rows(256), rows(256), rows(512), rows(512), rows(128),
                  pl.BlockSpec(gm.shape, lambda i: (0, 0)),
                  pl.BlockSpec((tb, H_M, D_V_M, D_QK_M), lambda i: (i, 0, 0, 0)),
                  pl.BlockSpec((tb, H_M, D_QK_M), lambda i: (i, 0, 0)),
                  pl.BlockSpec((tb, H_M), lambda i: (i, 0))],
        out_specs=(rows(512),
                   pl.BlockSpec((tb, H_M, D_V_M, D_QK_M), lambda i: (i, 0, 0, 0)),
                   pl.BlockSpec((tb, H_M, D_QK_M), lambda i: (i, 0, 0)),
                   pl.BlockSpec((tb, H_M), lambda i: (i, 0))),
        out_shape=(jax.ShapeDtypeStruct((nb, 512), BF16),
                   jax.ShapeDtypeStruct(c.shape, F32),
                   jax.ShapeDtypeStruct(n.shape, F32),
                   jax.ShapeDtypeStruct(m.shape, F32)),
        compiler_params=_cparams(1),
        name="mlstm_sample",
    )(mq, mk, mv, mo, gl, gm, c, n, m)


_QROWS = 16


def _attn_decode_kernel(pt_ref, q_ref, ks_ref, vs_ref, lam_ref, gda_ref, *refs, n_pages):
    k_refs = refs[:n_pages]
    v_refs = refs[n_pages:2 * n_pages]
    o_ref = refs[2 * n_pages]
    row16 = lax.broadcasted_iota(jnp.int32, (_QROWS, 128), 0)
    lane16 = lax.broadcasted_iota(jnp.int32, (_QROWS, 128), 1)

    def per_map_rows(x):
        out = jnp.zeros((_QROWS, 128), F32)
        for h in range(H_DA):
            out = jnp.where(row16 // 2 == h, jnp.broadcast_to(x[h:h + 1], (_QROWS, 128)), out)
        return out

    own_map = (lane16 < D_QK_DA) == (row16 % 2 == 0)
    qz = jnp.where(own_map, per_map_rows(q_ref[...]), 0.0).astype(BF16)
    ks16 = per_map_rows(ks_ref[...].astype(BF16).astype(F32))
    vs16 = per_map_rows(vs_ref[...])
    s_self = jnp.sum(qz.astype(F32) * ks16, axis=1, keepdims=True)

    n_rows = k_refs[0].shape[0]
    rowi = lax.broadcasted_iota(jnp.int32, (_QROWS, n_rows), 0)
    coli = lax.broadcasted_iota(jnp.int32, (_QROWS, n_rows), 1)
    same_head = (coli % H_DA) == (rowi // 2)
    scores = [jnp.where(same_head, _dot_nt(qz, k_refs[j][...].astype(BF16)), -jnp.inf) for j in range(n_pages)]
    m = s_self
    for sj in scores:
        m = jnp.maximum(m, jnp.max(sj, axis=1, keepdims=True))
    p_self = jnp.exp(s_self - m)
    l = p_self
    acc = p_self * vs16
    for j in range(n_pages):
        p = jnp.exp(scores[j] - m)
        l = l + jnp.sum(p, axis=1, keepdims=True)
        acc = acc + _dot(p.astype(BF16), v_refs[j][...].astype(BF16))
    o = acc / l
    lam = _lambda_value(lam_ref)
    g = gda_ref[...]
    for h in range(H_DA):
        a = o[2 * h:2 * h + 1] - lam * o[2 * h + 1:2 * h + 2]
        o_ref[:, h * D_V_DA:(h + 1) * D_V_DA] = (_rms(a, g) * (1.0 - LAM_INIT)).astype(BF16)


def _attn_decode(page_table, q, ks, vs, lam_p, gda, cache_k, cache_v):
    nb = q.shape[0]
    n_pages = page_table.shape[0] // nb
    rows = cache_k.shape[1]
    tok = pl.BlockSpec((None, H_DA, 128), lambda b, pt: (b, 0, 0))
    full = lambda a: pl.BlockSpec(a.shape, lambda b, pt: (0,) * a.ndim)

    def page_spec(j):
        return pl.BlockSpec((None, rows, 128), lambda b, pt: (pt[b * n_pages + j], 0, 0))

    grid_spec = pltpu.PrefetchScalarGridSpec(
        num_scalar_prefetch=1,
        grid=(nb,),
        in_specs=[tok, tok, tok, full(lam_p), full(gda)]
                 + [page_spec(j) for j in range(n_pages)] + [page_spec(j) for j in range(n_pages)],
        out_specs=pl.BlockSpec((None, 1, 512), lambda b, pt: (b, 0, 0)),
    )
    return pl.pallas_call(
        functools.partial(_attn_decode_kernel, n_pages=n_pages),
        grid_spec=grid_spec,
        out_shape=jax.ShapeDtypeStruct((nb, 1, 512), BF16),
        compiler_params=_cparams(1),
        name="attn_decode",
    )(page_table, q, ks, vs, lam_p, gda, *([cache_k] * n_pages), *([cache_v] * n_pages))


_R_E1, _R_E2, _R_RANK1, _R_RANK2, _R_P1, _R_P2 = range(6)


def _route(logits, base):
    lane = lax.broadcasted_iota(jnp.int32, logits.shape, 1)
    big = jnp.int32(ROUTER_LANES)
    neg = -jnp.inf
    gl = jnp.where(lane < N_GROUPS, logits, neg)
    gmax = jnp.max(gl, axis=1, keepdims=True)
    g_idx = jnp.min(jnp.where(gl == gmax, lane, big), axis=1, keepdims=True)
    g_p = 1.0 / jnp.sum(jnp.exp(gl - gmax), axis=1, keepdims=True)
    lo = N_GROUPS + EXPERTS_PER_GROUP * g_idx
    in_group = (lane >= lo) & (lane < lo + EXPERTS_PER_GROUP)
    el = jnp.where(in_group, logits, neg)
    v1 = jnp.max(el, axis=1, keepdims=True)
    i1 = jnp.min(jnp.where(el == v1, lane, big), axis=1, keepdims=True)
    el2 = jnp.where(lane == i1, neg, el)
    v2 = jnp.max(el2, axis=1, keepdims=True)
    i2 = jnp.min(jnp.where(el2 == v2, lane, big), axis=1, keepdims=True)
    e2 = jnp.exp(v2 - v1)
    p1 = g_p / (1.0 + e2)
    p2 = g_p * e2 / (1.0 + e2)
    onehot = jnp.where(lane == i1, 1.0, jnp.where(lane == i2, 1.0, 0.0))
    tm = logits.shape[0]
    earlier = lax.broadcasted_iota(jnp.int32, (tm, tm), 1) < lax.broadcasted_iota(jnp.int32, (tm, tm), 0)
    ranks = _dot(jnp.where(earlier, 1.0, 0.0).astype(BF16), onehot.astype(BF16)) + base
    r1 = jnp.sum(jnp.where(lane == i1, ranks, 0.0), axis=1, keepdims=True)
    r2 = jnp.sum(jnp.where(lane == i2, ranks, 0.0), axis=1, keepdims=True)
    counts = base + jnp.sum(onehot, axis=0, keepdims=True)
    record = jnp.zeros(logits.shape, F32)
    for ln, val in ((_R_E1, (i1 - N_GROUPS).astype(F32)), (_R_E2, (i2 - N_GROUPS).astype(F32)),
                    (_R_RANK1, r1), (_R_RANK2, r2), (_R_P1, p1), (_R_P2, p2)):
        record = jnp.where(lane == ln, val, record)
    return record, counts


def _mixout_kernel(hm_ref, ha_ref, x_ref, wo_ref, gf_ref, wr_ref, br_ref, cnt_in_ref,
                   xmid_ref, h2_ref, route_ref, cnt_ref):
    @pl.when(pl.program_id(0) == 0)
    def _():
        cnt_ref[...] = cnt_in_ref[...]

    half = hm_ref.shape[1]
    y = _dot(hm_ref[...], wo_ref[0:half, :]) + _dot(ha_ref[...], wo_ref[half:2 * half, :])
    xmid = x_ref[...] + y
    xmid_ref[...] = xmid
    h2 = _rms(xmid, gf_ref[...])
    _store_row_tiles(h2_ref, h2)
    record, counts = _route(_dot(h2.astype(BF16), wr_ref[...]) + br_ref[...], cnt_ref[...])
    route_ref[...] = record
    cnt_ref[...] = counts


def _mixout(hm, ha, x, wo, gf, wr, br, cnt_in, tm):
    rows, d = x.shape
    rs = lambda c: pl.BlockSpec((tm, c), lambda i: (i, 0))
    full = lambda a: pl.BlockSpec(a.shape, lambda i: (0,) * a.ndim)
    return pl.pallas_call(
        _mixout_kernel,
        grid=(rows // tm,),
        in_specs=[rs(512), rs(512), rs(d), full(wo), full(gf), full(wr), full(br), full(cnt_in)],
        out_specs=(rs(d), pl.BlockSpec((tm * (d // 128), 128), lambda i: (i, 0)), rs(ROUTER_LANES), full(cnt_in)),
        out_shape=(jax.ShapeDtypeStruct((rows, d), F32),
                   jax.ShapeDtypeStruct((rows * (d // 128), 128), F32),
                   jax.ShapeDtypeStruct((rows, ROUTER_LANES), F32),
                   jax.ShapeDtypeStruct(cnt_in.shape, F32)),
        compiler_params=_cparams(1),
        name="mixout",
    )(hm, ha, x, wo, gf, wr, br, cnt_in)


_TM_E = 256
_NJ = 8


def _dispatch_kernel(fill_ref, pos_ref, h2_ref, *refs, tm, zero_fill):
    if zero_fill:
        xs_hbm, zero_ref, sem = refs
    else:
        _, xs_hbm, sem = refs

    if zero_fill:
        @pl.when(pl.program_id(0) == 0)
        def _():
            zero_ref[...] = jnp.zeros_like(zero_ref)
            fills = [pltpu.make_async_copy(zero_ref, xs_hbm.at[pl.ds(pl.multiple_of(fill_ref[e], _NJ), _TM_E * _NJ)],
                                           sem.at[1]) for e in range(N_EXPERTS)]
            for f in fills:
                f.start()
            for f in fills:
                f.wait()

    def issue(t, carry):
        for c in range(2):
            dst = pl.multiple_of(pos_ref[0, 2 * t + c], _NJ)
            pltpu.make_async_copy(h2_ref.at[pl.ds(pl.multiple_of(t * _NJ, _NJ), _NJ)],
                                  xs_hbm.at[pl.ds(dst, _NJ)], sem.at[0]).start(priority=c)
        return carry

    lax.fori_loop(0, tm, issue, 0, unroll=8)
    for c in range(2):
        pltpu.make_async_copy(h2_ref, xs_hbm.at[pl.ds(0, tm * _NJ)], sem.at[0]).wait()


def _dispatch(fill_start, pos, h2, xs, tm, slot_rows):
    rows = h2.shape[0] // _NJ
    zero_fill = xs is None
    any_spec = pl.BlockSpec(memory_space=pl.ANY)
    in_specs = [pl.BlockSpec((None, 1, 2 * tm), lambda i, fs: (i, 0, 0), memory_space=pltpu.SMEM),
                pl.BlockSpec((tm * _NJ, 128), lambda i, fs: (i, 0))]
    args = [fill_start, pos, h2]
    if not zero_fill:
        in_specs.append(any_spec)
        args.append(xs)
    grid_spec = pltpu.PrefetchScalarGridSpec(
        num_scalar_prefetch=1,
        grid=(rows // tm,),
        in_specs=in_specs,
        out_specs=any_spec,
        scratch_shapes=([pltpu.VMEM((_TM_E * _NJ, 128), F32)] if zero_fill else []) + [pltpu.SemaphoreType.DMA((2,))],
    )
    return pl.pallas_call(
        functools.partial(_dispatch_kernel, tm=tm, zero_fill=zero_fill),
        grid_spec=grid_spec,
        out_shape=jax.ShapeDtypeStruct((slot_rows * _NJ, 128), F32),
        input_output_aliases={} if zero_fill else {3: 0},
        compiler_params=_cparams(1),
        name="moe_dispatch",
    )(*args)


def _expert_kernel(te_ref, nu_ref, x_ref, wg_ref, wu_ref, wd_ref, o_ref, x_s, wg_s, wu_s, wd_s):
    i = pl.program_id(0)

    @pl.when(i < nu_ref[0])
    def _():
        @pl.when((i == 0) | (te_ref[i] != te_ref[jnp.maximum(i - 1, 0)]))
        def _():
            wg_s[...] = wg_ref[...].astype(BF16)
            wu_s[...] = wu_ref[...].astype(BF16)
            wd_s[...] = wd_ref[...].astype(BF16)

        for j in range(_NJ):
            x_s[:, j * 128:(j + 1) * 128] = _load_row_tile_cols(x_ref, j, _TM_E, _NJ).astype(BF16)
        x = x_s[...]
        g = _dot(x, wg_s[...])
        u = _dot(x, wu_s[...])
        act = (g * _sigmoid(g) * u).astype(BF16)
        _store_row_tiles(o_ref, _dot(act, wd_s[...]))


def _experts(tile_expert, n_used, xs, w_gate, w_up, w_down, n_tiles):
    n_e, d, d_e = w_gate.shape
    used = lambda i, te, nu: (jnp.minimum(i, nu[0] - 1), 0)
    grid_spec = pltpu.PrefetchScalarGridSpec(
        num_scalar_prefetch=2,
        grid=(n_tiles,),
        in_specs=[pl.BlockSpec((_TM_E * _NJ, 128), used),
                  pl.BlockSpec((None, d, d_e), lambda i, te, nu: (te[i], 0, 0)),
                  pl.BlockSpec((None, d, d_e), lambda i, te, nu: (te[i], 0, 0)),
                  pl.BlockSpec((None, d_e, d), lambda i, te, nu: (te[i], 0, 0))],
        out_specs=pl.BlockSpec((_TM_E * _NJ, 128), used),
        scratch_shapes=[pltpu.VMEM((_TM_E, d), BF16), pltpu.VMEM((d, d_e), BF16), pltpu.VMEM((d, d_e), BF16),
                        pltpu.VMEM((d_e, d), BF16)],
    )
    return pl.pallas_call(
        _expert_kernel,
        grid_spec=grid_spec,
        out_shape=jax.ShapeDtypeStruct(xs.shape, F32),
        compiler_params=_cparams(1),
        name="moe_experts",
    )(tile_expert, n_used, xs, w_gate, w_up, w_down)


def _combine_kernel(pos_ref, posn_ref, route_ref, xmid_ref, gfin_ref, o_hbm, y_ref, buf, sem, *, tm):
    i = pl.program_id(0)
    n = pl.num_programs(0)
    d = xmid_ref.shape[1]

    def start_gather(p_ref, slot):
        def body(t, carry):
            for c in range(2):
                src = pl.multiple_of(p_ref[0, 2 * t + c], _NJ)
                pltpu.make_async_copy(o_hbm.at[pl.ds(src, _NJ)],
                                      buf.at[slot, c, pl.ds(pl.multiple_of(t * _NJ, _NJ), _NJ)],
                                      sem.at[slot]).start(priority=c)
            return carry
        lax.fori_loop(0, tm, body, 0, unroll=8)

    def finish(slot):
        for c in range(2):
            pltpu.make_async_copy(o_hbm.at[pl.ds(0, tm * _NJ)], buf.at[slot, c], sem.at[slot]).wait()
        route = route_ref[...]
        lane = lax.broadcasted_iota(jnp.int32, route.shape, 1)
        p1 = jnp.sum(jnp.where(lane == _R_P1, route, 0.0), axis=1, keepdims=True)
        p2 = jnp.sum(jnp.where(lane == _R_P2, route, 0.0), axis=1, keepdims=True)
        ssq = jnp.zeros((tm, 1), F32)
        for j in range(d // 128):
            cols = slice(j * 128, (j + 1) * 128)
            yj = (xmid_ref[:, cols] + p1 * _load_row_tile_cols(buf.at[slot, 0], j, tm, _NJ)
                  + p2 * _load_row_tile_cols(buf.at[slot, 1], j, tm, _NJ))
            y_ref[:, cols] = yj
            ssq = ssq + jnp.sum(yj * yj, axis=1, keepdims=True)
        y_ref[...] = y_ref[...] * lax.rsqrt(ssq / d + EPS) * gfin_ref[...]

    @pl.when(i == 0)
    def _():
        start_gather(pos_ref, 0)

    for slot in range(2):
        @pl.when(i % 2 == slot)
        def _(slot=slot):
            @pl.when(i + 1 < n)
            def _():
                start_gather(posn_ref, 1 - slot)
            finish(slot)


def _combine(pos, route, xmid, gfin, o, tm):
    rows, d = xmid.shape
    n = rows // tm
    grid_spec = pl.GridSpec(
        grid=(n,),
        in_specs=[pl.BlockSpec((None, 1, 2 * tm), lambda i: (i, 0, 0), memory_space=pltpu.SMEM),
                  pl.BlockSpec((None, 1, 2 * tm), lambda i: (jnp.minimum(i + 1, n - 1), 0, 0),
                               memory_space=pltpu.SMEM),
                  pl.BlockSpec((tm, ROUTER_LANES), lambda i: (i, 0)),
                  pl.BlockSpec((tm, d), lambda i: (i, 0)),
                  pl.BlockSpec(gfin.shape, lambda i: (0, 0)),
                  pl.BlockSpec(memory_space=pl.ANY)],
        out_specs=pl.BlockSpec((tm, d), lambda i: (i, 0)),
        scratch_shapes=[pltpu.VMEM((2, 2, tm * _NJ, 128), F32), pltpu.SemaphoreType.DMA((2,))],
    )
    return pl.pallas_call(
        functools.partial(_combine_kernel, tm=tm),
        grid_spec=grid_spec,
        out_shape=jax.ShapeDtypeStruct((rows, d), F32),
        compiler_params=_cparams(1),
        name="moe_combine",
    )(pos, pos, route, xmid, gfin, o)


def kernel(x_prompt, x_sample, cache_k, cache_v, state_C, state_n, state_m, page_table, meta_tokens, norm_mix, w_in, b_gates, head_norm_m, lambda_q1, lambda_k1, lambda_q2, lambda_k2, head_norm_da, w_out, norm_ffn, w_group, b_group, w_router, b_router, w_gate_e, w_up_e, w_down_e, norm_final):
    n_batch, seq, d_model = x_prompt.shape
    nb_dec = x_sample.shape[0]
    assert cache_k.shape[0] == 1 and x_sample.shape[1] == 1, "one layer, one new token per sequence"
    n_pages = page_table.shape[1]

    w = w_in[0]
    cuts = [0]
    for wd_ in (256, 256, 512, 512, 4, 4, 512, 512, 512):
        cuts.append(cuts[-1] + wd_)
    w_mq, w_mk, w_mv, w_mo, w_mi, w_mf, w_dq, w_dk, w_dv = (w[:, cuts[i]:cuts[i + 1]] for i in range(9))
    w_gates = jnp.concatenate([w_mi, w_mf], axis=1)
    w_p = jnp.concatenate([w_mq, w_mv, w_mo, w_dq, w_dk, w_dv], axis=1).astype(BF16)
    wkT = w_mk.T.astype(BF16)
    wgT = jnp.pad(w_gates.T, ((0, 8), (0, 0))).astype(BF16)
    w_s = jnp.concatenate([w_mq, w_mk, w_mv, w_mo, w_dq, w_dk, w_dv,
                           jnp.pad(w_gates, ((0, 0), (0, 120)))], axis=1).astype(BF16)
    bg_col = b_gates[0].reshape(8, 1).astype(F32)
    bg_row = jnp.pad(b_gates[0].reshape(1, 8), ((0, 0), (0, 120))).astype(F32)
    g_mix = norm_mix[0].reshape(1, d_model)
    gm = head_norm_m[0].reshape(1, D_V_M)
    gda = head_norm_da[0].reshape(1, D_V_DA)
    lam_p = jnp.stack([lambda_q1[0], lambda_k1[0], lambda_q2[0], lambda_k2[0]]).astype(F32)
    wo = w_out[0].astype(BF16)
    g_ffn = norm_ffn[0].reshape(1, d_model)
    w_r = jnp.pad(jnp.concatenate([w_group[0], w_router[0]], axis=1),
                  ((0, 0), (0, ROUTER_LANES - N_GROUPS - N_EXPERTS))).astype(BF16)
    b_r = jnp.pad(jnp.concatenate([b_group[0], b_router[0]]).reshape(1, -1),
                  ((0, 0), (0, ROUTER_LANES - N_GROUPS - N_EXPERTS))).astype(F32)
    g_fin = norm_final.reshape(1, d_model)

    xp = x_prompt.reshape(n_batch * seq, d_model)
    (mq, mv, mo, dq, dk, dv, dkb, dvb, kT, gl) = _inproj_prompt(xp, g_mix, w_p, wkT, wgT, bg_col, n_batch, seq, 512)
    x_meta = jnp.pad(meta_tokens.astype(F32), ((0, CHUNK - N_META), (0, 0)))
    (mq_m, mv_m, mo_m, _, dk_m, dv_m, dkb_m, dvb_m, kT_m, gl_m) = _inproj_prompt(
        x_meta, g_mix, w_p, wkT, wgT, bg_col, 1, CHUNK, CHUNK)
    (mq_s, mk_s, mv_s, mo_s, dq_s, dk_s, dv_s, gl_s) = _inproj_sample(
        x_sample.reshape(nb_dec, d_model), g_mix, w_s, bg_row)

    lane = jnp.arange(CHUNK)[None, None, :]
    pad_gate = jnp.where(jnp.arange(8)[None, :, None] < H_M, PAD_LOG_GATE, 0.0)
    gl_m = jnp.where(lane < N_META, gl_m, pad_gate)
    c_zero = jnp.zeros((1, H_M, D_QK_M, 256), F32)
    m_zero = jnp.zeros((1, H_M, CHUNK), F32)
    _, c_meta, m_meta = _mlstm_prompt(mq_m.reshape(1, CHUNK, 256), kT_m, mv_m.reshape(1, CHUNK, 512),
                                      mo_m.reshape(1, CHUNK, 512), gl_m, gm, c_zero, m_zero)
    hm_p, c_fin, m_fin = _mlstm_prompt(
        mq.reshape(n_batch, seq, 256), kT, mv.reshape(n_batch, seq, 512), mo.reshape(n_batch, seq, 512), gl, gm,
        jnp.broadcast_to(c_meta, (n_batch,) + c_meta.shape[1:]),
        jnp.broadcast_to(m_meta, (n_batch,) + m_meta.shape[1:]))
    hm_s, c_s, n_s, m_s = _mlstm_sample(mq_s, mk_s, mv_s, mo_s, gl_s, gm,
                                        state_C[0].astype(F32), state_n[0].astype(F32), state_m[0].astype(F32))

    ha_p = _attn_prompt(dq.reshape(n_batch, seq, 512), dkb.reshape(n_batch, seq, 512),
                        dvb.reshape(n_batch, seq, 512), dkb_m[:N_META], dvb_m[:N_META], lam_p, gda, 256, 512)
    n_pool, page = cache_k.shape[1], cache_k.shape[2]
    ha_s = _attn_decode(page_table.reshape(-1).astype(jnp.int32),
                        dq_s.reshape(nb_dec, H_DA, 128), dk_s.reshape(nb_dec, H_DA, 128),
                        dv_s.reshape(nb_dec, H_DA, 128), lam_p, gda,
                        cache_k.reshape(n_pool, page * H_DA, 128), cache_v.reshape(n_pool, page * H_DA, 128))

    n_prompt_rows = n_batch * seq
    cnt0 = jnp.zeros((1, ROUTER_LANES), F32)
    xmid_p, h2_p, route_p, cnt_p = _mixout(hm_p.reshape(n_prompt_rows, 512), ha_p.reshape(n_prompt_rows, 512), xp,
                                           wo, g_ffn, w_r, b_r, cnt0, 512)
    xmid_s, h2_s, route_s, cnt_all = _mixout(hm_s, ha_s.reshape(nb_dec, 512), x_sample.reshape(nb_dec, d_model),
                                             wo, g_ffn, w_r, b_r, cnt_p, nb_dec)

    n_tiles = (2 * (n_prompt_rows + nb_dec)) // _TM_E + N_EXPERTS
    slot_rows = (n_tiles + 1) * _TM_E
    counts = cnt_all[0, N_GROUPS:N_GROUPS + N_EXPERTS].astype(jnp.int32)
    padded = ((counts + _TM_E - 1) // _TM_E) * _TM_E
    ends = jnp.cumsum(padded)
    starts = ends - padded
    n_used = (ends[-1:] // _TM_E).astype(jnp.int32)
    tile_first_row = jnp.arange(n_tiles, dtype=jnp.int32) * _TM_E
    tile_expert = jnp.minimum(jnp.sum(tile_first_row[:, None] >= ends[None, :], axis=1), N_EXPERTS - 1)
    fill_start = ((starts + counts) * _NJ).astype(jnp.int32)

    def slots(route, tm):
        e = route[:, _R_E1:_R_E2 + 1].astype(jnp.int32)
        r = route[:, _R_RANK1:_R_RANK2 + 1].astype(jnp.int32)
        return ((starts[e] + r) * _NJ).astype(jnp.int32).reshape(route.shape[0] // tm, 1, 2 * tm)

    tm_c = 256
    pos_p = slots(route_p, tm_c)
    pos_s = slots(route_s, nb_dec)
    xs = _dispatch(fill_start, pos_p, h2_p, None, tm_c, slot_rows)
    xs = _dispatch(fill_start, pos_s, h2_s, xs, nb_dec, slot_rows)
    eo = _experts(tile_expert.astype(jnp.int32), n_used, xs, w_gate_e[0], w_up_e[0], w_down_e[0], n_tiles)
    y_p = _combine(pos_p, route_p, xmid_p, g_fin, eo, tm_c)
    y_s = _combine(pos_s, route_s, xmid_s, g_fin, eo, nb_dec)

    def with_meta(meta_rows, real_rows):
        meta_b = jnp.broadcast_to(meta_rows[None, :N_META * H_DA], (n_batch, N_META * H_DA, 128))
        full = jnp.concatenate([meta_b, real_rows.reshape(n_batch, seq * H_DA, 128)], axis=1)
        return full.reshape(1, n_batch, N_META + seq, H_DA, 128)

    k_prompt = with_meta(dk_m, dk).astype(cache_k.dtype)
    v_prompt = with_meta(dv_m, dv).astype(cache_v.dtype)
    C_prompt = jnp.swapaxes(c_fin[:, :, :, :D_V_M], 2, 3)[None].astype(state_C.dtype)
    n_prompt = c_fin[:, :, :, D_V_M][None].astype(state_n.dtype)
    m_prompt = m_fin[:, :, 0][None].astype(state_m.dtype)
    return (y_p.reshape(n_batch, seq, d_model).astype(x_prompt.dtype),
            y_s.reshape(nb_dec, 1, d_model).astype(x_sample.dtype),
            k_prompt, v_prompt, C_prompt, n_prompt, m_prompt,
            dk_s.reshape(1, nb_dec, 1, H_DA, 128).astype(cache_k.dtype),
            dv_s.reshape(1, nb_dec, 1, H_DA, 128).astype(cache_v.dtype),
            c_s[None].astype(state_C.dtype), n_s[None].astype(state_n.dtype), m_s[None].astype(state_m.dtype))
```
